```python
import math, functools
import jax, jax.numpy as jnp
from jax import lax
import numpy as np

D_MODEL = 2048
BATCH = 2
SEQ = 8192
DEPTH = 1
DEC_BATCH = 32
DEC_SEQ = 4
PAST_LEN = 16384
PAGE_SIZE = 128

HG_HEADS = 8
HG_DK = 128
HG_DV = 128
HGRN_CHUNK = 64
ATT_HEADS = 8
HEAD_DIM = 128
MOBA_BLOCK = 256
MOBA_TOPK = 3
Q_BLOCK = 64
MIX_WIDTH = HG_HEADS * HG_DV + ATT_HEADS * HEAD_DIM
IN_WIDTH = 2 * HG_HEADS * HG_DK + 2 * HG_HEADS * HG_DV + 3 * ATT_HEADS * HEAD_DIM
N_GROUPS = 4
EXPERTS_PER_GROUP = 8
EXPERT_TOPK = 2
EXPERT_FF = 256
EPS = 1e-6

kernel_name = 'hymba_hgrn2_moba_hmoe_step'


def rmsnorm(x, w):
    xf = x.astype(jnp.float32)
    y = xf * lax.rsqrt(jnp.mean(xf * xf, axis=-1, keepdims=True) + EPS)
    return (y * w.astype(jnp.float32)).astype(x.dtype)


def hgrn2_chunked(q, k, logf, v, s0):
    B, T, H, DK = q.shape
    C = HGRN_CHUNK if T % HGRN_CHUNK == 0 else T
    n = T // C

    def chunks(a):
        return a.reshape(B, n, C, H, a.shape[-1]).transpose(1, 0, 3, 2, 4)

    causal = jnp.tril(jnp.ones((C, C), bool))[None, None, :, :, None]

    def step(S, inp):
        qc, kc, gc, vc = inp
        b = jnp.cumsum(gc, axis=2)
        decay = jnp.exp(jnp.where(causal, b[:, :, :, None, :] - b[:, :, None, :, :], -jnp.inf))
        attn = jnp.einsum('bhtd,bhsd,bhtsd->bhts', qc, kc, decay)
        o = jnp.einsum('bhts,bhsv->bhtv', attn, vc) + jnp.einsum('bhtd,bhdv->bhtv', qc * jnp.exp(b), S)
        b_last = b[:, :, -1:, :]
        S = jnp.exp(b_last[:, :, 0, :, None]) * S + jnp.einsum('bhsd,bhsv->bhdv', kc * jnp.exp(b_last - b), vc)
        return S, o

    S, o = lax.scan(step, s0, (chunks(q), chunks(k), chunks(logf), chunks(v)))
    o = o.transpose(1, 0, 3, 2, 4).reshape(B, T, H, v.shape[-1])
    return o, S


def moba_prompt(q, k, v):
    B, S, H, Dh = q.shape
    nb = -(-S // MOBA_BLOCK)
    pad = nb * MOBA_BLOCK - S
    kp = jnp.pad(k, ((0, 0), (0, pad), (0, 0), (0, 0)))
    vp = jnp.pad(v, ((0, 0), (0, pad), (0, 0), (0, 0)))
    kb = kp.reshape(B, nb, MOBA_BLOCK, H, Dh).transpose(0, 3, 1, 2, 4)
    vb = vp.reshape(B, nb, MOBA_BLOCK, H, Dh).transpose(0, 3, 1, 2, 4)
    kmean = jnp.mean(kb.astype(jnp.float32), axis=3)
    n_sel = min(MOBA_TOPK, nb - 1)
    scale = HEAD_DIM ** -0.5
    nq = S // Q_BLOCK
    qc = q.reshape(B, nq, Q_BLOCK, H, Dh).transpose(1, 0, 3, 2, 4)
    starts = jnp.arange(nq, dtype=jnp.int32) * Q_BLOCK
    bi = jnp.arange(B)[:, None, None, None]
    hi = jnp.arange(H)[None, :, None, None]
    blk_ids = jnp.arange(nb)
    offs = jnp.arange(MOBA_BLOCK)

    def one(args):
        qs, q0 = args
        own = q0 // MOBA_BLOCK
        qpos = q0 + jnp.arange(Q_BLOCK)
        scores, vals, specs = [], [], []
        if n_sel > 0:
            gate = jnp.einsum('bhqd,bhnd->bhqn', qs.astype(jnp.float32), kmean)
            gate = jnp.where(blk_ids < own, gate, -jnp.inf)
            top_val, top_idx = lax.top_k(gate, n_sel)
            valid = top_val > -jnp.inf
            k_sel = kb[bi, hi, top_idx]
            v_sel = vb[bi, hi, top_idx]
            s = jnp.einsum('bhqd,bhqkjd->bhqkj', qs, k_sel).astype(jnp.float32) * scale
            s = jnp.where(valid[..., None], s, -jnp.inf)
            scores.append(s.reshape(B, H, Q_BLOCK, n_sel * MOBA_BLOCK))
            vals.append(v_sel.reshape(B, H, Q_BLOCK, n_sel * MOBA_BLOCK, Dh))
            specs.append('bhqj,bhqjd->bhqd')
        k_own = lax.dynamic_index_in_dim(kb, own, axis=2, keepdims=False)
        v_own = lax.dynamic_index_in_dim(vb, own, axis=2, keepdims=False)
        s_own = jnp.einsum('bhqd,bhjd->bhqj', qs, k_own).astype(jnp.float32) * scale
        kpos = own * MOBA_BLOCK + offs
        s_own = jnp.where(kpos[None, :] <= qpos[:, None], s_own, -jnp.inf)
        scores.append(s_own)
        vals.append(v_own)
        specs.append('bhqj,bhjd->bhqd')
        p = jax.nn.softmax(jnp.concatenate(scores, axis=-1), axis=-1)
        sizes = np.cumsum([s_.shape[-1] for s_ in scores])[:-1].tolist()
        parts = jnp.split(p, sizes, axis=-1)
        out = jnp.einsum(specs[0], parts[0].astype(vals[0].dtype), vals[0])
        for spec, pp, vv in zip(specs[1:], parts[1:], vals[1:]):
            out = out + jnp.einsum(spec, pp.astype(vv.dtype), vv)
        return out

    o = lax.map(one, (qc, starts))
    return o.transpose(1, 0, 3, 2, 4).reshape(B, S, H, Dh)


def moba_sample(q, k, v, cache_k, cache_v, page_table):
    T = q.shape[1]
    n_pages = page_table.shape[1]
    past = n_pages * PAGE_SIZE
    ppb = MOBA_BLOCK // PAGE_SIZE
    n_full = past // MOBA_BLOCK
    tail = past - n_full * MOBA_BLOCK
    n_sel = min(MOBA_TOPK, n_full)
    scale = HEAD_DIM ** -0.5
    causal = jnp.tril(jnp.ones((T, T), bool))
    head_idx = jnp.arange(ATT_HEADS)[:, None, None]

    def one(args):
        qs, kn, vn, pt = args
        qh = qs.transpose(1, 0, 2)
        rows_k = cache_k[pt].reshape(past, ATT_HEADS, HEAD_DIM)
        scores, vals, specs = [], [], []
        if n_sel > 0:
            full_k = rows_k[: n_full * MOBA_BLOCK].reshape(n_full, MOBA_BLOCK, ATT_HEADS, HEAD_DIM).transpose(2, 0, 1, 3)
            kmean = jnp.mean(full_k.astype(jnp.float32), axis=2)
            gate = jnp.einsum('htd,hnd->htn', qh.astype(jnp.float32), kmean)
            _, idx = lax.top_k(gate, n_sel)
            k_sel = full_k[head_idx, idx]
            pages = pt[idx[..., None] * ppb + jnp.arange(ppb)]
            v_sel = cache_v[pages, :, head_idx[..., None]]
            s = jnp.einsum('htd,htkjd->htkj', qh, k_sel).astype(jnp.float32) * scale
            scores.append(s.reshape(ATT_HEADS, T, n_sel * MOBA_BLOCK))
            vals.append(v_sel.reshape(ATT_HEADS, T, n_sel * MOBA_BLOCK, HEAD_DIM))
            specs.append('htj,htjd->htd')
        if tail > 0:
            k_tail = rows_k[n_full * MOBA_BLOCK:].transpose(1, 0, 2)
            v_tail = cache_v[pt[n_full * ppb:]].reshape(tail, ATT_HEADS, HEAD_DIM).transpose(1, 0, 2)
            scores.append(jnp.einsum('htd,hjd->htj', qh, k_tail).astype(jnp.float32) * scale)
            vals.append(v_tail)
            specs.append('htj,hjd->htd')
        s_new = jnp.einsum('htd,hjd->htj', qh, kn.transpose(1, 0, 2)).astype(jnp.float32) * scale
        scores.append(jnp.where(causal, s_new, -jnp.inf))
        vals.append(vn.transpose(1, 0, 2))
        specs.append('htj,hjd->htd')
        p = jax.nn.softmax(jnp.concatenate(scores, axis=-1), axis=-1)
        sizes = np.cumsum([s_.shape[-1] for s_ in scores])[:-1].tolist()
        parts = jnp.split(p, sizes, axis=-1)
        out = jnp.einsum(specs[0], parts[0].astype(vals[0].dtype), vals[0])
        for spec, pp, vv in zip(specs[1:], parts[1:], vals[1:]):
            out = out + jnp.einsum(spec, pp.astype(vv.dtype), vv)
        return out.transpose(1, 0, 2)

    return lax.map(one, (q, k, v, page_table))


def hier_moe(h, w_group, w_expert, w_gate, w_up, w_down):
    lead = h.shape[:-1]
    hf = h.reshape(-1, D_MODEL)
    n = hf.shape[0]
    g_logit = (hf @ w_group).astype(jnp.float32)
    g_prob = jax.nn.softmax(g_logit, axis=-1)
    g_sel = jnp.argmax(g_logit, axis=-1)
    p_group = jnp.take_along_axis(g_prob, g_sel[:, None], axis=-1)
    e_logit = (hf @ w_expert).astype(jnp.float32).reshape(n, N_GROUPS, EXPERTS_PER_GROUP)
    e_logit = jnp.take_along_axis(e_logit, g_sel[:, None, None], axis=1)[:, 0]
    e_prob = jax.nn.softmax(e_logit, axis=-1)
    top_p, top_i = lax.top_k(e_prob, EXPERT_TOPK)
    top_p = top_p / jnp.sum(top_p, axis=-1, keepdims=True)
    within = jnp.sum(jax.nn.one_hot(top_i, EXPERTS_PER_GROUP, dtype=jnp.float32) * top_p[..., None], axis=-2)
    combine = jax.nn.one_hot(g_sel, N_GROUPS, dtype=jnp.float32)[:, :, None] * (p_group * within)[:, None, :]
    out = None
    for g in range(N_GROUPS):
        a = jnp.einsum('nd,edf->nef', hf, w_gate[g])
        b = jnp.einsum('nd,edf->nef', hf, w_up[g])
        act = (jax.nn.silu(a) * b * combine[:, g, :, None].astype(hf.dtype)).astype(hf.dtype)
        term = jnp.einsum('nef,efd->nd', act, w_down[g])
        out = term if out is None else out + term
    return out.reshape(lead + (D_MODEL,))


def decoder_layer(x, s0, attend, norm_mix, w_in, lb, hgrn_norm, w_out, norm_ffn,
                  w_group, w_expert, w_gate, w_up, w_down):
    B, T, _ = x.shape
    hk = HG_HEADS * HG_DK
    hv = HG_HEADS * HG_DV
    aw = ATT_HEADS * HEAD_DIM
    h = rmsnorm(x, norm_mix)
    proj = h @ w_in
    splits = [hk, 2 * hk, 2 * hk + hv, 2 * hk + 2 * hv, 2 * hk + 2 * hv + aw, 2 * hk + 2 * hv + 2 * aw]
    hq, hf, hi, hg, aq, ak, av = jnp.split(proj, splits, axis=-1)
    lbf = lb.astype(jnp.float32)
    hf32 = hf.astype(jnp.float32)
    logf = jnp.log(lbf + (1.0 - lbf) * jax.nn.sigmoid(hf32))
    k_in = (1.0 - lbf) * jax.nn.sigmoid(-hf32)
    q_h = jax.nn.silu(hq.astype(jnp.float32))
    o_h, s_new = hgrn2_chunked(q_h.reshape(B, T, HG_HEADS, HG_DK), k_in.reshape(B, T, HG_HEADS, HG_DK),
                               logf.reshape(B, T, HG_HEADS, HG_DK),
                               hi.astype(jnp.float32).reshape(B, T, HG_HEADS, HG_DV), s0.astype(jnp.float32))
    o_h = rmsnorm(o_h, hgrn_norm.reshape(HG_HEADS, HG_DV)).reshape(B, T, hv)
    o_h = (o_h * jax.nn.silu(hg.astype(jnp.float32))).astype(x.dtype)
    aq = aq.reshape(B, T, ATT_HEADS, HEAD_DIM)
    ak = ak.reshape(B, T, ATT_HEADS, HEAD_DIM)
    av = av.reshape(B, T, ATT_HEADS, HEAD_DIM)
    o_a = attend(aq, ak, av).reshape(B, T, aw).astype(x.dtype)
    x = x + jnp.concatenate([o_h, o_a], axis=-1) @ w_out
    x = x + hier_moe(rmsnorm(x, norm_ffn), w_group, w_expert, w_gate, w_up, w_down)
    return x, ak, av, s_new


def setup_inputs(seed: int = 0) -> dict:
    key = jax.random.key(seed)
    ks = jax.random.split(key, 20)
    f32 = jnp.float32
    n_pages = PAST_LEN // PAGE_SIZE
    n_pool = (5 * DEC_BATCH * n_pages + 3) // 4
    E = EXPERTS_PER_GROUP
    return {
        'x_prompt': jax.random.normal(ks[0], (BATCH, SEQ, D_MODEL), f32),
        'x_sample': jax.random.normal(ks[1], (DEC_BATCH, DEC_SEQ, D_MODEL), f32),
        'cache_k': jax.random.normal(ks[2], (DEPTH, n_pool, PAGE_SIZE, ATT_HEADS, HEAD_DIM), f32),
        'cache_v': jax.random.normal(ks[3], (DEPTH, n_pool, PAGE_SIZE, ATT_HEADS, HEAD_DIM), f32),
        'state_hgrn': 0.4 * jax.random.normal(ks[4], (DEPTH, DEC_BATCH, HG_HEADS, HG_DK, HG_DV), f32),
        'page_table': jax.random.permutation(ks[5], n_pool)[: DEC_BATCH * n_pages].reshape(DEC_BATCH, n_pages).astype(jnp.int32),
        'norm_mix': 1.0 + 0.01 * jax.random.normal(ks[6], (DEPTH, D_MODEL), f32),
        'w_in': jax.random.normal(ks[7], (DEPTH, D_MODEL, IN_WIDTH), f32) * D_MODEL ** -0.5,
        'lb_logits': 0.1 * jax.random.normal(ks[8], (DEPTH + 1, HG_HEADS * HG_DK), f32),
        'hgrn_norm': 1.0 + 0.01 * jax.random.normal(ks[9], (DEPTH, HG_HEADS * HG_DV), f32),
        'w_out': jax.random.normal(ks[10], (DEPTH, MIX_WIDTH, D_MODEL), f32) * MIX_WIDTH ** -0.5,
        'norm_ffn': 1.0 + 0.01 * jax.random.normal(ks[11], (DEPTH, D_MODEL), f32),
        'w_group': jax.random.normal(ks[12], (DEPTH, D_MODEL, N_GROUPS), f32) * D_MODEL ** -0.5,
        'w_expert': jax.random.normal(ks[13], (DEPTH, D_MODEL, N_GROUPS * E), f32) * D_MODEL ** -0.5,
        'w_gate': jax.random.normal(ks[14], (DEPTH, N_GROUPS, E, D_MODEL, EXPERT_FF), f32) * D_MODEL ** -0.5,
        'w_up': jax.random.normal(ks[15], (DEPTH, N_GROUPS, E, D_MODEL, EXPERT_FF), f32) * D_MODEL ** -0.5,
        'w_down': jax.random.normal(ks[16], (DEPTH, N_GROUPS, E, EXPERT_FF, D_MODEL), f32) * EXPERT_FF ** -0.5,
        'norm_final': 1.0 + 0.01 * jax.random.normal(ks[17], (D_MODEL,), f32),
    }


def reference(x_prompt, x_sample, cache_k, cache_v, state_hgrn, page_table, norm_mix, w_in, lb_logits,
              hgrn_norm, w_out, norm_ffn, w_group, w_expert, w_gate, w_up, w_down, norm_final):
    lb_all = jnp.cumsum(jax.nn.softmax(lb_logits.astype(jnp.float32), axis=0), axis=0)
    xp, xs = x_prompt, x_sample
    kp_l, vp_l, sp_l, ks_l, vs_l, ss_l = [], [], [], [], [], []
    for layer in range(DEPTH):
        w = (norm_mix[layer], w_in[layer], lb_all[layer], hgrn_norm[layer], w_out[layer], norm_ffn[layer],
             w_group[layer], w_expert[layer], w_gate[layer], w_up[layer], w_down[layer])
        s0 = jnp.zeros((xp.shape[0], HG_HEADS, HG_DK, HG_DV), jnp.float32)
        xp, kp, vp, sp = decoder_layer(xp, s0, moba_prompt, *w)
        attend_s = functools.partial(moba_sample, cache_k=cache_k[layer], cache_v=cache_v[layer],
                                     page_table=page_table)
        xs, k_s, v_s, s_s = decoder_layer(xs, state_hgrn[layer], attend_s, *w)
        kp_l.append(kp); vp_l.append(vp); sp_l.append(sp)
        ks_l.append(k_s); vs_l.append(v_s); ss_l.append(s_s)
    y_prompt = rmsnorm(xp, norm_final)
    y_sample = rmsnorm(xs, norm_final)
    return (y_prompt, y_sample, jnp.stack(kp_l), jnp.stack(vp_l), jnp.stack(sp_l),
            jnp.stack(ks_l), jnp.stack(vs_l), jnp.stack(ss_l))
```

```python
import functools

import jax
import jax.numpy as jnp
from jax import lax
from jax.experimental import pallas as pl
from jax.experimental.pallas import tpu as pltpu

F32 = jnp.float32
BF16 = jnp.bfloat16
HIGHEST = lax.Precision.HIGHEST

D_MODEL = 2048
HEADS = 8
HEAD_W = 128
GROUP_W = HEADS * HEAD_W
MAIN_W = 5 * GROUP_W
HGRN_CHUNK = 64
HGRN_SUB = 16
MOBA_BLOCK = 256
MOBA_TOPK = 3
PAGE_SIZE = 128
N_GROUPS = 4
EXPERTS_PER_GROUP = 8
N_EXPERTS = N_GROUPS * EXPERTS_PER_GROUP
EXPERT_FF = 256
EPS = 1e-6
LANES = 128
MOE_TILE = 256
VMEM_LIMIT = 52 * 1024 * 1024
NEG_BIG = -1e30

_NT = (((1,), (1,)), ((), ()))
_TN = (((0,), (0,)), ((), ()))


def _params(*sem):
    return pltpu.CompilerParams(dimension_semantics=sem, vmem_limit_bytes=VMEM_LIMIT)


def _mm(a, b, dims=None, *, precise):
    if dims is None:
        dims = (((a.ndim - 1,), (0,)), ((), ()))
    if precise:
        return lax.dot_general(a.astype(F32), b.astype(F32), dims, precision=HIGHEST, preferred_element_type=F32)
    return lax.dot_general(a.astype(BF16), b.astype(BF16), dims, preferred_element_type=F32)


def _inproj_kernel(x_ref, nw_ref, w_ref, main_ref, k_ref, v_ref, h_scr, *, n_main, n_kv, precise):
    j = pl.program_id(1)

    @pl.when(j == 0)
    def _():
        x = x_ref[...]
        ms = jnp.mean(x * x, axis=-1, keepdims=True)
        h_scr[...] = (x * lax.rsqrt(ms + EPS) * nw_ref[...]).astype(h_scr.dtype)

    acc = _mm(h_scr[...], w_ref[...], precise=precise)

    @pl.when(j < n_main)
    def _():
        main_ref[...] = acc

    @pl.when((j >= n_main) & (j < n_main + n_kv))
    def _():
        k_ref[...] = acc

    @pl.when(j >= n_main + n_kv)
    def _():
        v_ref[...] = acc


def _inproj(x, norm_w, w_in):
    n = x.shape[0]
    precise = w_in.dtype == F32
    tm = min(n, 1024)
    tn = 512
    n_main = MAIN_W // tn
    n_kv = GROUP_W // tn
    grid = (n // tm, n_main + 2 * n_kv)
    return pl.pallas_call(
        functools.partial(_inproj_kernel, n_main=n_main, n_kv=n_kv, precise=precise),
        grid=grid,
        in_specs=[
            pl.BlockSpec((tm, D_MODEL), lambda i, j: (i, 0)),
            pl.BlockSpec((1, D_MODEL), lambda i, j: (0, 0)),
            pl.BlockSpec((D_MODEL, tn), lambda i, j: (0, j)),
        ],
        out_specs=[
            pl.BlockSpec((tm, tn), lambda i, j: (i, jnp.minimum(j, n_main - 1))),
            pl.BlockSpec((tm, tn), lambda i, j: (i, jnp.clip(j - n_main, 0, n_kv - 1))),
            pl.BlockSpec((tm, tn), lambda i, j: (i, jnp.clip(j - n_main - n_kv, 0, n_kv - 1))),
        ],
        out_shape=[
            jax.ShapeDtypeStruct((n, MAIN_W), F32),
            jax.ShapeDtypeStruct((n, GROUP_W), F32),
            jax.ShapeDtypeStruct((n, GROUP_W), F32),
        ],
        scratch_shapes=[pltpu.VMEM((tm, D_MODEL), w_in.dtype)],
        compiler_params=_params("arbitrary", "arbitrary"),
        name="inproj",
    )(x, norm_w, w_in)


def _hgrn_head(hq, hf, hi, lbv, st, tri, *, chunk, sub, t_valid, precise):
    mm = functools.partial(_mm, precise=precise)
    sig = jax.nn.sigmoid(hf)
    g = jnp.log(lbv + (1.0 - lbv) * sig)
    k = (1.0 - lbv) * jax.nn.sigmoid(-hf)
    q = hq * jax.nn.sigmoid(hq)
    v = hi
    if t_valid < chunk:
        row = lax.broadcasted_iota(jnp.int32, (chunk, HEAD_W), 0)
        live = row < t_valid
        g = jnp.where(live, g, 0.0)
        k = jnp.where(live, k, 0.0)
    b = jnp.dot(tri, g, precision=HIGHEST, preferred_element_type=F32)
    vb = v if precise else v.astype(BF16)
    parts = []
    for blk in range(chunk // sub):
        r0, r1 = blk * sub, (blk + 1) * sub
        if blk == 0:
            qt = q[r0:r1] * jnp.exp(b[r0:r1])
            kt = k[0:r1] * jnp.exp(-b[0:r1])
        else:
            beta = b[r0 - 1:r0]
            qt = q[r0:r1] * jnp.exp(b[r0:r1] - beta)
            kt = k[0:r1] * jnp.exp(beta - b[0:r1])
        a = mm(qt, kt, _NT)
        trow = lax.broadcasted_iota(jnp.int32, (sub, r1), 0) + r0
        scol = lax.broadcasted_iota(jnp.int32, (sub, r1), 1)
        a = jnp.where(scol <= trow, a, 0.0)
        parts.append(mm(a, vb[0:r1]))
    o = parts[0] if len(parts) == 1 else jnp.concatenate(parts, axis=0)
    o = o + mm(q * jnp.exp(b), st, _NT)
    b_last = b[chunk - 1:chunk]
    st_new = st * jnp.exp(b_last) + mm(vb, k * jnp.exp(b_last - b), _TN)
    return o, st_new


def _hgrn_kernel(*refs, chunk, sub, t_valid, has_s0, precise):
    if has_s0:
        q_ref, f_ref, i_ref, g_ref, lb_ref, nw_ref, s0_ref, o_ref, s_ref, st_scr = refs
    else:
        q_ref, f_ref, i_ref, g_ref, lb_ref, nw_ref, o_ref, s_ref, st_scr = refs
        s0_ref = None
    c = pl.program_id(1)

    @pl.when(c == 0)
    def _():
        if has_s0:
            for h in range(HEADS):
                st_scr[h] = s0_ref[0, h].T
        else:
            st_scr[...] = jnp.zeros_like(st_scr)

    row = lax.broadcasted_iota(jnp.int32, (chunk, chunk), 0)
    col = lax.broadcasted_iota(jnp.int32, (chunk, chunk), 1)
    tri = (col <= row).astype(F32)
    for h in range(HEADS):
        sl = slice(h * HEAD_W, (h + 1) * HEAD_W)
        o, st_new = _hgrn_head(q_ref[:, sl], f_ref[:, sl], i_ref[:, sl], lb_ref[:, sl], st_scr[h], tri,
                               chunk=chunk, sub=sub, t_valid=t_valid, precise=precise)
        st_scr[h] = st_new
        ms = jnp.mean(o * o, axis=-1, keepdims=True)
        o = o * lax.rsqrt(ms + EPS) * nw_ref[:, sl]
        hg = g_ref[:, sl]
        o_ref[:, sl] = (o * (hg * jax.nn.sigmoid(hg))).astype(o_ref.dtype)

    @pl.when(c == pl.num_programs(1) - 1)
    def _():
        for h in range(HEADS):
            s_ref[0, h] = st_scr[h].T


def _hgrn(main, lbv, norm_w, s0, *, batch, chunk, sub, t_valid, precise):
    n = main.shape[0]
    n_chunks = n // batch // chunk
    slab = lambda s: pl.BlockSpec((chunk, GROUP_W), lambda b, c, s=s: (b * n_chunks + c, s))
    vec = pl.BlockSpec((1, GROUP_W), lambda b, c: (0, 0))
    state = pl.BlockSpec((1, HEADS, HEAD_W, HEAD_W), lambda b, c: (b, 0, 0, 0))
    in_specs = [slab(0), slab(1), slab(2), slab(3), vec, vec]
    args = [main, main, main, main, lbv, norm_w]
    if s0 is not None:
        in_specs.append(state)
        args.append(s0)
    return pl.pallas_call(
        functools.partial(_hgrn_kernel, chunk=chunk, sub=sub, t_valid=t_valid, has_s0=s0 is not None,
                          precise=precise),
        grid=(batch, n_chunks),
        in_specs=in_specs,
        out_specs=[pl.BlockSpec((chunk, GROUP_W), lambda b, c: (b * n_chunks + c, 0)), state],
        out_shape=[jax.ShapeDtypeStruct((n, GROUP_W), F32 if precise else BF16),
                   jax.ShapeDtypeStruct((batch, HEADS, HEAD_W, HEAD_W), F32)],
        scratch_shapes=[pltpu.VMEM((HEADS, HEAD_W, HEAD_W), F32)],
        compiler_params=_params("arbitrary", "arbitrary"),
        name="hgrn",
    )(*args)


def _top_blocks(gate, n_sel):
    nb = gate.shape[0]
    blk = lax.broadcasted_iota(jnp.int32, gate.shape, 0)
    sel = jnp.zeros(gate.shape, F32)
    for _ in range(n_sel):
        mx = jnp.max(gate, axis=0, keepdims=True)
        first = jnp.min(jnp.where(gate == mx, blk, nb), axis=0, keepdims=True)
        hit = blk == first
        sel = jnp.where(hit & (mx > -jnp.inf), 1.0, sel)
        gate = jnp.where(hit, -jnp.inf, gate)
    return sel


def _moba_prompt_kernel(q_ref, k_ref, v_ref, o_ref, kb_scr, vt_scr, kmean_scr, sel_scr, *, nb, n_sel):
    i = pl.program_id(2)
    blk = MOBA_BLOCK
    scale = HEAD_W ** -0.5

    @pl.when(i == 0)
    def _():
        def prep(n, carry):
            rows = pl.ds(pl.multiple_of(n * blk, blk), blk)
            kblk = k_ref[rows, :]
            kb_scr[n] = kblk.astype(BF16)
            kmean_scr[pl.ds(n, 1), :] = jnp.mean(kblk, axis=0, keepdims=True)
            vt_scr[n] = v_ref[rows, :].T.astype(BF16)
            return carry
        lax.fori_loop(0, nb, prep, 0)

    q = q_ref[...]
    qb = q.astype(BF16)
    if n_sel > 0:
        gate = lax.dot_general(kmean_scr[...], q, _NT, precision=HIGHEST, preferred_element_type=F32)
        bid = lax.broadcasted_iota(jnp.int32, gate.shape, 0)
        gate = jnp.where(bid < i, gate, -jnp.inf)
        sel_scr[...] = _top_blocks(gate, n_sel)

    s = lax.dot_general(kb_scr[i], qb, _NT, preferred_element_type=F32) * scale
    kpos = lax.broadcasted_iota(jnp.int32, s.shape, 0)
    qpos = lax.broadcasted_iota(jnp.int32, s.shape, 1)
    s = jnp.where(kpos <= qpos, s, NEG_BIG)
    m = jnp.max(s, axis=0, keepdims=True)
    p = jnp.exp(s - m)
    l = jnp.sum(p, axis=0, keepdims=True)
    acc = jnp.dot(vt_scr[i], p.astype(BF16), preferred_element_type=F32)

    def body(n, carry):
        m, l, acc = carry
        picked = sel_scr[pl.ds(n, 1), :] > 0.5
        s = lax.dot_general(kb_scr[n], qb, _NT, preferred_element_type=F32) * scale
        m_new = jnp.where(picked, jnp.maximum(m, jnp.max(s, axis=0, keepdims=True)), m)
        p = jnp.exp(s - jnp.where(picked, m_new, -NEG_BIG))
        alpha = jnp.exp(m - m_new)
        l = alpha * l + jnp.sum(p, axis=0, keepdims=True)
        acc = alpha * acc + jnp.dot(vt_scr[n], p.astype(BF16), preferred_element_type=F32)
        return m_new, l, acc

    if n_sel > 0:
        m, l, acc = lax.fori_loop(0, i, body, (m, l, acc))
    o_ref[...] = (acc / l).T.astype(BF16)


def _moba_prompt(main, k, v, *, batch):
    n = main.shape[0]
    seq = n // batch
    nb = seq // MOBA_BLOCK
    n_sel = min(MOBA_TOPK, nb - 1)
    q_col0 = 4 * HEADS
    return pl.pallas_call(
        functools.partial(_moba_prompt_kernel, nb=nb, n_sel=n_sel),
        grid=(batch, HEADS, nb),
        in_specs=[
            pl.BlockSpec((MOBA_BLOCK, HEAD_W), lambda b, h, i: (b * nb + i, q_col0 + h)),
            pl.BlockSpec((seq, HEAD_W), lambda b, h, i: (b, h)),
            pl.BlockSpec((seq, HEAD_W), lambda b, h, i: (b, h)),
        ],
        out_specs=pl.BlockSpec((MOBA_BLOCK, HEAD_W), lambda b, h, i: (b * nb + i, h)),
        out_shape=jax.ShapeDtypeStruct((n, GROUP_W), BF16),
        scratch_shapes=[
            pltpu.VMEM((nb, MOBA_BLOCK, HEAD_W), BF16),
            pltpu.VMEM((nb, HEAD_W, MOBA_BLOCK), BF16),
            pltpu.VMEM((max(nb, 8), HEAD_W), F32),
            pltpu.VMEM((max(nb, 8), MOBA_BLOCK), F32),
        ],
        compiler_params=_params("arbitrary", "arbitrary", "arbitrary"),
        name="moba_prompt",
    )(main, k, v)


def _route(logits):
    lane = lax.broadcasted_iota(jnp.int32, logits.shape, 1)
    is_g = lane < N_GROUPS
    gl = jnp.where(is_g, logits, -jnp.inf)
    gmax = jnp.max(gl, axis=-1, keepdims=True)
    g_sel = jnp.min(jnp.where(gl == gmax, lane, LANES), axis=-1, keepdims=True)
    p_group = 1.0 / jnp.sum(jnp.where(is_g, jnp.exp(gl - gmax), 0.0), axis=-1, keepdims=True)
    e_lo = N_GROUPS + EXPERTS_PER_GROUP * g_sel
    is_e = (lane >= e_lo) & (lane < e_lo + EXPERTS_PER_GROUP)
    el = jnp.where(is_e, logits, -jnp.inf)
    emax = jnp.max(el, axis=-1, keepdims=True)
    ee = jnp.where(is_e, jnp.exp(el - emax), 0.0)
    prob = ee / jnp.sum(ee, axis=-1, keepdims=True)
    prob = jnp.where(is_e, prob, -1.0)
    p1 = jnp.max(prob, axis=-1, keepdims=True)
    i1 = jnp.min(jnp.where(prob == p1, lane, LANES), axis=-1, keepdims=True)
    prob2 = jnp.where(lane == i1, -1.0, prob)
    p2 = jnp.max(prob2, axis=-1, keepdims=True)
    i2 = jnp.min(jnp.where(prob2 == p2, lane, LANES), axis=-1, keepdims=True)
    tot = p1 + p2
    w1 = p_group * (p1 / tot)
    w2 = p_group * (p2 / tot)
    e1 = (i1 - N_GROUPS).astype(F32)
    e2 = (i2 - N_GROUPS).astype(F32)
    return jnp.where(lane == 0, e1, jnp.where(lane == 1, e2, jnp.where(lane == 2, w1, jnp.where(lane == 3, w2, 0.0))))


def _outproj_kernel(x_ref, oh_ref, oa_ref, w_ref, nw_ref, wr_ref, x1_ref, hn_ref, route_ref, *, precise):
    y = _mm(oh_ref[...], w_ref[0:GROUP_W, :], precise=precise)
    y = y + _mm(oa_ref[...], w_ref[GROUP_W:2 * GROUP_W, :], precise=precise)
    x1 = x_ref[...] + y
    x1_ref[...] = x1
    ms = jnp.mean(x1 * x1, axis=-1, keepdims=True)
    hn = x1 * lax.rsqrt(ms + EPS) * nw_ref[...]
    hn_ref[...] = hn
    logits = jnp.dot(hn, wr_ref[...], precision=HIGHEST, preferred_element_type=F32)
    route_ref[...] = _route(logits)


def _outproj(x, oh, oa, w_out, norm_w, w_router):
    n = x.shape[0]
    tm = min(n, 256)
    row = lambda w: pl.BlockSpec((tm, w), lambda i: (i, 0))
    full = lambda a, b: pl.BlockSpec((a, b), lambda i: (0, 0))
    return pl.pallas_call(
        functools.partial(_outproj_kernel, precise=w_out.dtype == F32),
        grid=(n // tm,),
        in_specs=[row(D_MODEL), row(GROUP_W), row(GROUP_W), full(2 * GROUP_W, D_MODEL), full(1, D_MODEL),
                  full(D_MODEL, LANES)],
        out_specs=[row(D_MODEL), row(D_MODEL), row(LANES)],
        out_shape=[jax.ShapeDtypeStruct((n, D_MODEL), F32), jax.ShapeDtypeStruct((n, D_MODEL), F32),
                   jax.ShapeDtypeStruct((n, LANES), F32)],
        compiler_params=_params("arbitrary"),
        name="outproj",
    )(x, oh, oa, w_out, norm_w, w_router)


def _moe_kernel(te_ref, tv_ref, rt_ref, hn_hbm, rw_ref, wg_ref, wu_ref, wd_ref, y_ref, xbuf, sem):
    t = pl.program_id(0)
    nt = pl.num_programs(0)
    slot = t % 2

    def row_copy(tile, r, slot):
        tok = rt_ref[tile * MOE_TILE + r]
        return pltpu.make_async_copy(hn_hbm.at[pl.ds(tok, 1)], xbuf.at[slot, pl.ds(r, 1)], sem.at[slot])

    def issue(tile, slot):
        def body(r, carry):
            row_copy(tile, r, slot).start()
            return carry
        lax.fori_loop(0, MOE_TILE, body, 0, unroll=8)

    @pl.when(t == 0)
    def _():
        issue(0, 0)

    @pl.when(t + 1 < nt)
    def _():
        issue(t + 1, 1 - slot)

    def wait_body(r, carry):
        row_copy(t, r, slot).wait()
        return carry
    lax.fori_loop(0, MOE_TILE, wait_body, 0, unroll=8)

    @pl.when(tv_ref[t] > 0)
    def _():
        x = xbuf[slot].astype(BF16)
        a = jnp.dot(x, wg_ref[0], preferred_element_type=F32)
        b = jnp.dot(x, wu_ref[0], preferred_element_type=F32)
        rw = rw_ref[...]
        act = (a * jax.nn.sigmoid(a)) * b * jnp.concatenate([rw] * (EXPERT_FF // LANES), axis=1)
        y_ref[...] = jnp.dot(act.astype(BF16), wd_ref[0], preferred_element_type=F32)

    @pl.when(tv_ref[t] == 0)
    def _():
        y_ref[...] = jnp.zeros_like(y_ref)


def _moe(hn, tile_expert, tile_valid, row_token, row_w, wg, wu, wd):
    n_rows = row_token.shape[0]
    n_tiles = n_rows // MOE_TILE
    grid_spec = pltpu.PrefetchScalarGridSpec(
        num_scalar_prefetch=3,
        grid=(n_tiles,),
        in_specs=[
            pl.BlockSpec(memory_space=pl.ANY),
            pl.BlockSpec((MOE_TILE, LANES), lambda t, te, tv, rt: (t, 0)),
            pl.BlockSpec((1, D_MODEL, EXPERT_FF), lambda t, te, tv, rt: (te[t], 0, 0)),
            pl.BlockSpec((1, D_MODEL, EXPERT_FF), lambda t, te, tv, rt: (te[t], 0, 0)),
            pl.BlockSpec((1, EXPERT_FF, D_MODEL), lambda t, te, tv, rt: (te[t], 0, 0)),
        ],
        out_specs=pl.BlockSpec((MOE_TILE, D_MODEL), lambda t, te, tv, rt: (t, 0)),
        scratch_shapes=[pltpu.VMEM((2, MOE_TILE, D_MODEL), F32), pltpu.SemaphoreType.DMA((2,))],
    )
    return pl.pallas_call(
        _moe_kernel,
        grid_spec=grid_spec,
        out_shape=jax.ShapeDtypeStruct((n_rows, D_MODEL), F32),
        compiler_params=_params("arbitrary"),
        name="moe",
    )(tile_expert, tile_valid, row_token, hn, row_w, wg, wu, wd)


def _moe_plan(route, n_rows):
    n = route.shape[0]
    ids = route[:, 0:2].astype(jnp.int32).reshape(-1)
    wts = route[:, 2:4].reshape(-1)
    onehot = (ids[:, None] == jnp.arange(N_EXPERTS, dtype=jnp.int32)[None, :]).astype(jnp.int32)
    csum = jnp.cumsum(onehot, axis=0)
    rank = jnp.take_along_axis(csum, ids[:, None], axis=1)[:, 0] - 1
    counts = csum[-1]
    padded = ((counts + MOE_TILE - 1) // MOE_TILE) * MOE_TILE
    ends = jnp.cumsum(padded)
    starts = ends - padded
    dest = starts[ids] + rank
    token = jnp.arange(2 * n, dtype=jnp.int32) // 2
    row_token = jnp.zeros((n_rows,), jnp.int32).at[dest].set(token)
    row_w = jnp.zeros((n_rows,), F32).at[dest].set(wts)
    row_w = jnp.broadcast_to(row_w[:, None], (n_rows, LANES))
    tile_start = jnp.arange(n_rows // MOE_TILE, dtype=jnp.int32) * MOE_TILE
    tile_expert = jnp.sum((tile_start[:, None] >= ends[None, :]).astype(jnp.int32), axis=1)
    tile_valid = (tile_expert < N_EXPERTS).astype(jnp.int32)
    last = jnp.max(jnp.where(counts > 0, jnp.arange(N_EXPERTS, dtype=jnp.int32), 0))
    tile_expert = jnp.where(tile_valid > 0, tile_expert, last).astype(jnp.int32)
    return tile_expert, tile_valid, row_token, row_w, dest


def _combine_kernel(dest_ref, x1_ref, y_hbm, nw_ref, o_ref, ybuf, sem, *, tm):
    t = pl.program_id(0)
    nt = pl.num_programs(0)
    slot = t % 2

    def row_copy(tile, r, slot):
        src = dest_ref[tile * (2 * tm) + r]
        return pltpu.make_async_copy(y_hbm.at[pl.ds(src, 1)], ybuf.at[slot, pl.ds(r, 1)], sem.at[slot])

    def issue(tile, slot):
        def body(r, carry):
            row_copy(tile, r, slot).start()
            return carry
        lax.fori_loop(0, 2 * tm, body, 0, unroll=8)

    @pl.when(t == 0)
    def _():
        issue(0, 0)

    @pl.when(t + 1 < nt)
    def _():
        issue(t + 1, 1 - slot)

    def wait_body(r, carry):
        row_copy(t, r, slot).wait()
        return carry
    lax.fori_loop(0, 2 * tm, wait_body, 0, unroll=8)

    x = x1_ref[...] + ybuf[slot, 0:tm, :] + ybuf[slot, tm:2 * tm, :]
    ms = jnp.mean(x * x, axis=-1, keepdims=True)
    o_ref[...] = x * lax.rsqrt(ms + EPS) * nw_ref[...]


def _combine(x1, y_sorted, dest_tiled, norm_w):
    n = x1.shape[0]
    tm = min(n, 128)
    grid_spec = pltpu.PrefetchScalarGridSpec(
        num_scalar_prefetch=1,
        grid=(n // tm,),
        in_specs=[
            pl.BlockSpec((tm, D_MODEL), lambda t, d: (t, 0)),
            pl.BlockSpec(memory_space=pl.ANY),
            pl.BlockSpec((1, D_MODEL), lambda t, d: (0, 0)),
        ],
        out_specs=pl.BlockSpec((tm, D_MODEL), lambda t, d: (t, 0)),
        scratch_shapes=[pltpu.VMEM((2, 2 * tm, D_MODEL), F32), pltpu.SemaphoreType.DMA((2,))],
    )
    return pl.pallas_call(
        functools.partial(_combine_kernel, tm=tm),
        grid_spec=grid_spec,
        out_shape=jax.ShapeDtypeStruct((n, D_MODEL), F32),
        compiler_params=_params("arbitrary"),
        name="combine",
    )(dest_tiled, x1, y_sorted, norm_w)


def _ffn_and_final(x1, hn, route, wg, wu, wd, norm_final):
    n = x1.shape[0]
    n_rows = 2 * n + N_EXPERTS * MOE_TILE
    tile_expert, tile_valid, row_token, row_w, dest = _moe_plan(route, n_rows)
    y_sorted = _moe(hn, tile_expert, tile_valid, row_token, row_w, wg, wu, wd)
    tm = min(n, 128)
    dest_tiled = dest.reshape(n // tm, tm, 2).transpose(0, 2, 1).reshape(-1)
    return _combine(x1, y_sorted, dest_tiled, norm_final)


def _kmean_kernel(pt_ref, *refs, pages_per_step):
    page_refs = refs[:pages_per_step]
    o_ref = refs[pages_per_step]
    j = pl.program_id(1)
    ppb = MOBA_BLOCK // PAGE_SIZE
    for blk in range(pages_per_step // ppb):
        tot = jnp.sum(page_refs[blk * ppb][0], axis=0, keepdims=True)
        for pg in range(1, ppb):
            tot = tot + jnp.sum(page_refs[blk * ppb + pg][0], axis=0, keepdims=True)
        o_ref[0, pl.ds(j * (pages_per_step // ppb) + blk, 1), :] = tot * (1.0 / MOBA_BLOCK)


def _kmean(cache_k3, page_table):
    dec_batch, n_pages = page_table.shape
    ppb = MOBA_BLOCK // PAGE_SIZE
    n_blocks = n_pages // ppb
    pages_per_step = 8
    steps = n_pages // pages_per_step

    def page_spec(p):
        return pl.BlockSpec((1, PAGE_SIZE, GROUP_W),
                            lambda s, j, pt, p=p: (pt[s * n_pages + j * pages_per_step + p], 0, 0))

    grid_spec = pltpu.PrefetchScalarGridSpec(
        num_scalar_prefetch=1,
        grid=(dec_batch, steps),
        in_specs=[page_spec(p) for p in range(pages_per_step)],
        out_specs=pl.BlockSpec((1, n_blocks, GROUP_W), lambda s, j, pt: (s, 0, 0)),
    )
    return pl.pallas_call(
        functools.partial(_kmean_kernel, pages_per_step=pages_per_step),
        grid_spec=grid_spec,
        out_shape=jax.ShapeDtypeStruct((dec_batch, n_blocks, GROUP_W), F32),
        compiler_params=_params("arbitrary", "arbitrary"),
        name="kmean",
    )(page_table.reshape(-1), *([cache_k3] * pages_per_step))


def _sample_topk_kernel(q_ref, kmean_ref, idx_ref, q_scr, *, t_new, n_sel):
    q_scr[...] = jnp.zeros_like(q_scr)
    for h in range(HEADS):
        sl = slice(h * HEAD_W, (h + 1) * HEAD_W)
        q_scr[0:t_new, :] = q_ref[0, :, sl]
        gate = lax.dot_general(kmean_ref[0, :, sl], q_scr[...], _NT, precision=HIGHEST,
                               preferred_element_type=F32)
        nb = gate.shape[0]
        blk = lax.broadcasted_iota(jnp.int32, gate.shape, 0)
        row = lax.broadcasted_iota(jnp.int32, (8, LANES), 0)
        tile = jnp.zeros((8, LANES), jnp.int32)
        for r in range(n_sel):
            mx = jnp.max(gate, axis=0, keepdims=True)
            first = jnp.min(jnp.where(gate == mx, blk, nb), axis=0, keepdims=True)
            tile = jnp.where(row == r, jnp.broadcast_to(first, (8, LANES)), tile)
            gate = jnp.where(blk == first, -jnp.inf, gate)
        idx_ref[0, h] = tile


def _sample_topk(main3, kmean, *, n_sel):
    dec_batch, t_new, _ = main3.shape
    n_blocks = kmean.shape[1]
    return pl.pallas_call(
        functools.partial(_sample_topk_kernel, t_new=t_new, n_sel=n_sel),
        grid=(dec_batch,),
        in_specs=[
            pl.BlockSpec((1, t_new, GROUP_W), lambda s: (s, 0, 4)),
            pl.BlockSpec((1, n_blocks, GROUP_W), lambda s: (s, 0, 0)),
        ],
        out_specs=pl.BlockSpec((1, HEADS, 8, LANES), lambda s: (s, 0, 0, 0)),
        out_shape=jax.ShapeDtypeStruct((dec_batch, HEADS, 8, LANES), jnp.int32),
        scratch_shapes=[pltpu.VMEM((LANES, HEAD_W), F32)],
        compiler_params=_params("arbitrary"),
        name="sample_topk",
    )(main3, kmean)


def _sample_attn_kernel(idx_ref, pt_ref, q_ref, kn_ref, vn_ref, ck_hbm, cv_hbm, o_ref,
                        kbuf, vbuf, q_scr, kn_scr, vn_scr, sem, *, t_new, n_sel, n_pages):
    s = pl.program_id(0)
    h = pl.program_id(1)
    step = s * HEADS + h
    n_steps = pl.num_programs(0) * HEADS
    slot = step % 2
    ppb = MOBA_BLOCK // PAGE_SIZE
    scale = HEAD_W ** -0.5

    def copies(step, slot):
        s_ = step // HEADS
        h_ = step % HEADS
        out = []
        for t in range(t_new):
            for r in range(n_sel):
                b = idx_ref[(step * t_new + t) * n_sel + r]
                for pg in range(ppb):
                    page = pt_ref[s_ * n_pages + b * ppb + pg]
                    rows = pl.ds((r * ppb + pg) * PAGE_SIZE, PAGE_SIZE)
                    out.append(pltpu.make_async_copy(ck_hbm.at[page, :, h_, :], kbuf.at[slot, t, rows, :],
                                                     sem.at[slot]))
                    out.append(pltpu.make_async_copy(cv_hbm.at[page, :, h_, :], vbuf.at[slot, t, rows, :],
                                                     sem.at[slot]))
        return out

    @pl.when(step == 0)
    def _():
        for c in copies(0, 0):
            c.start()

    @pl.when(step + 1 < n_steps)
    def _():
        for c in copies(step + 1, 1 - slot):
            c.start()

    q_scr[...] = jnp.zeros_like(q_scr)
    kn_scr[...] = jnp.zeros_like(kn_scr)
    vn_scr[...] = jnp.zeros_like(vn_scr)
    q_scr[0:t_new, :] = q_ref[0]
    kn_scr[0:t_new, :] = kn_ref[0]
    vn_scr[0:t_new, :] = vn_ref[0]
    mm = functools.partial(_mm, precise=True)
    qb = q_scr[...]
    s_new = mm(qb, kn_scr[...], _NT) * scale
    trow = lax.broadcasted_iota(jnp.int32, s_new.shape, 0)
    jcol = lax.broadcasted_iota(jnp.int32, s_new.shape, 1)
    new_ok = (jcol <= trow) & (jcol < t_new)
    s_new = jnp.where(new_ok, s_new, NEG_BIG)
    m_new = jnp.max(s_new, axis=-1, keepdims=True)

    for c in copies(step, slot):
        c.wait()

    orow = lax.broadcasted_iota(jnp.int32, (8, HEAD_W), 0)
    out = jnp.zeros((8, HEAD_W), F32)
    for t in range(t_new):
        st = mm(qb, kbuf[slot, t], _NT) * scale
        m = jnp.maximum(jnp.max(st, axis=-1, keepdims=True), m_new)
        p = jnp.exp(st - m)
        pn = jnp.where(new_ok, jnp.exp(s_new - m), 0.0)
        den = jnp.sum(p, axis=-1, keepdims=True) + jnp.sum(pn, axis=-1, keepdims=True)
        o_t = mm(p, vbuf[slot, t]) + mm(pn, vn_scr[...])
        out = jnp.where(orow == t, o_t / den, out)
    o_ref[0] = out[0:t_new, :]


def _sample_attn(idx, page_table, main3, kn3, vn3, cache_k4, cache_v4, *, n_sel):
    dec_batch, t_new, _ = main3.shape
    n_pages = page_table.shape[1]
    q_col0 = 4 * HEADS
    grid_spec = pltpu.PrefetchScalarGridSpec(
        num_scalar_prefetch=2,
        grid=(dec_batch, HEADS),
        in_specs=[
            pl.BlockSpec((1, t_new, HEAD_W), lambda s, h, ix, pt: (s, 0, q_col0 + h)),
            pl.BlockSpec((1, t_new, HEAD_W), lambda s, h, ix, pt: (s, 0, h)),
            pl.BlockSpec((1, t_new, HEAD_W), lambda s, h, ix, pt: (s, 0, h)),
            pl.BlockSpec(memory_space=pl.ANY),
            pl.BlockSpec(memory_space=pl.ANY),
        ],
        out_specs=pl.BlockSpec((1, t_new, HEAD_W), lambda s, h, ix, pt: (s, 0, h)),
        scratch_shapes=[
            pltpu.VMEM((2, t_new, n_sel * MOBA_BLOCK, HEAD_W), F32),
            pltpu.VMEM((2, t_new, n_sel * MOBA_BLOCK, HEAD_W), F32),
            pltpu.VMEM((8, HEAD_W), F32),
            pltpu.VMEM((16, HEAD_W), F32),
            pltpu.VMEM((16, HEAD_W), F32),
            pltpu.SemaphoreType.DMA((2,)),
        ],
    )
    return pl.pallas_call(
        functools.partial(_sample_attn_kernel, t_new=t_new, n_sel=n_sel, n_pages=n_pages),
        grid_spec=grid_spec,
        out_shape=jax.ShapeDtypeStruct((dec_batch, t_new, GROUP_W), F32),
        compiler_params=_params("arbitrary", "arbitrary"),
        name="sample_attn",
    )(idx, page_table.reshape(-1), main3, kn3, vn3, cache_k4, cache_v4)


def _moba_sample(main, kn, vn, cache_k, cache_v, page_table):
    dec_batch, n_pages = page_table.shape
    t_new = main.shape[0] // dec_batch
    n_pool = cache_k.shape[0]
    n_blocks = n_pages * PAGE_SIZE // MOBA_BLOCK
    n_sel = min(MOBA_TOPK, n_blocks)
    main3 = main.reshape(dec_batch, t_new, MAIN_W)
    kn3 = kn.reshape(dec_batch, t_new, GROUP_W)
    vn3 = vn.reshape(dec_batch, t_new, GROUP_W)
    kmean = _kmean(cache_k.reshape(n_pool, PAGE_SIZE, GROUP_W), page_table)
    idx = _sample_topk(main3, kmean, n_sel=n_sel)
    idx = idx[:, :, :n_sel, :t_new].transpose(0, 1, 3, 2).reshape(-1)
    o = _sample_attn(idx, page_table, main3, kn3, vn3, cache_k, cache_v, n_sel=n_sel)
    return o.reshape(dec_batch * t_new, GROUP_W)


def _layer(x, attend, hgrn_fn, weights):
    norm_mix, w_in, lbv, hgrn_norm, w_out, norm_ffn, w_router, wg, wu, wd, norm_final = weights
    main, k, v = _inproj(x, norm_mix, w_in)
    oh, state = hgrn_fn(main, lbv, hgrn_norm)
    oa = attend(main, k, v)
    x1, hn, route = _outproj(x, oh, oa, w_out, norm_ffn, w_router)
    y = _ffn_and_final(x1, hn, route, wg, wu, wd, norm_final)
    return y, k, v, state


def kernel(x_prompt, x_sample, cache_k, cache_v, state_hgrn, page_table, norm_mix, w_in, lb_logits, hgrn_norm,
           w_out, norm_ffn, w_group, w_expert, w_gate, w_up, w_down, norm_final):
    depth = w_in.shape[0]
    assert depth == 1, "the final norm is fused into the (single) layer"
    batch, seq, _ = x_prompt.shape
    dec_batch, t_new, _ = x_sample.shape
    lb_all = jnp.cumsum(jax.nn.softmax(lb_logits.astype(F32), axis=0), axis=0)
    layer = 0
    w_router = jnp.concatenate(
        [w_group[layer], w_expert[layer], jnp.zeros((D_MODEL, LANES - N_GROUPS - N_EXPERTS), F32)], axis=1)
    def weights(mix_dtype):
        return (
            norm_mix[layer][None, :], w_in[layer].astype(mix_dtype), lb_all[layer][None, :],
            hgrn_norm[layer][None, :], w_out[layer].astype(mix_dtype), norm_ffn[layer][None, :], w_router,
            w_gate[layer].reshape(N_EXPERTS, D_MODEL, EXPERT_FF).astype(BF16),
            w_up[layer].reshape(N_EXPERTS, D_MODEL, EXPERT_FF).astype(BF16),
            w_down[layer].reshape(N_EXPERTS, EXPERT_FF, D_MODEL).astype(BF16),
            norm_final[None, :],
        )

    hgrn_p = functools.partial(_hgrn, s0=None, batch=batch, chunk=HGRN_CHUNK, sub=HGRN_SUB, t_valid=HGRN_CHUNK,
                               precise=False)
    moba_p = functools.partial(_moba_prompt, batch=batch)
    yp, kp, vp, sp = _layer(x_prompt.reshape(batch * seq, D_MODEL), moba_p, hgrn_p, weights(BF16))

    pad = 8

    def hgrn_s(main, lbv, nw):
        m3 = main.reshape(dec_batch, t_new, MAIN_W)
        m3 = jnp.pad(m3, ((0, 0), (0, pad - t_new), (0, 0))).reshape(dec_batch * pad, MAIN_W)
        o, st = _hgrn(m3, lbv, nw, state_hgrn[layer], batch=dec_batch, chunk=pad, sub=pad, t_valid=t_new,
                      precise=True)
        return o.reshape(dec_batch, pad, GROUP_W)[:, :t_new].reshape(dec_batch * t_new, GROUP_W), st

    moba_s = functools.partial(_moba_sample, cache_k=cache_k[layer], cache_v=cache_v[layer], page_table=page_table)
    ys, ks, vs, ss = _layer(x_sample.reshape(dec_batch * t_new, D_MODEL), moba_s, hgrn_s, weights(F32))

    return (yp.reshape(batch, seq, D_MODEL), ys.reshape(dec_batch, t_new, D_MODEL),
            kp.reshape(1, batch, seq, HEADS, HEAD_W), vp.reshape(1, batch, seq, HEADS, HEAD_W), sp[None],
            ks.reshape(1, dec_batch, t_new, HEADS, HEAD_W), vs.reshape(1, dec_batch, t_new, HEADS, HEAD_W), ss[None])
```

```python
import functools

import jax
import jax.numpy as jnp
from jax import lax
from jax.experimental import pallas as pl
from jax.experimental.pallas import tpu as pltpu

F32 = jnp.float32
BF16 = jnp.bfloat16
HIGHEST = lax.Precision.HIGHEST

D_MODEL = 2048
HEADS = 8
HEAD_W = 128
GROUP_W = HEADS * HEAD_W
MAIN_W = 5 * GROUP_W
HGRN_CHUNK = 64
HGRN_SUB = 16
MOBA_BLOCK = 256
MOBA_TOPK = 3
PAGE_SIZE = 128
N_GROUPS = 4
EXPERTS_PER_GROUP = 8
N_EXPERTS = N_GROUPS * EXPERTS_PER_GROUP
EXPERT_FF = 256
EPS = 1e-6
LANES = 128
MOE_TILE = 256
VMEM_LIMIT = 52 * 1024 * 1024
NEG_BIG = -1e30

_NT = (((1,), (1,)), ((), ()))
_TN = (((0,), (0,)), ((), ()))


def _params(*sem):
    return pltpu.CompilerParams(dimension_semantics=sem, vmem_limit_bytes=VMEM_LIMIT)


def _mm(a, b, dims=None, *, precise):
    if dims is None:
        dims = (((a.ndim - 1,), (0,)), ((), ()))
    if precise:
        return lax.dot_general(a.astype(F32), b.astype(F32), dims, precision=HIGHEST, preferred_element_type=F32)
    return lax.dot_general(a.astype(BF16), b.astype(BF16), dims, preferred_element_type=F32)


def _mm3(a, b, dims=None):
    if dims is None:
        dims = (((a.ndim - 1,), (0,)), ((), ()))

    def halves(x):
        hi = x.astype(BF16)
        return hi, (x - hi.astype(F32)).astype(BF16)

    (ah, al), (bh, bl) = halves(a), halves(b)
    dot = lambda x, y: lax.dot_general(x, y, dims, preferred_element_type=F32)
    return dot(ah, bh) + (dot(ah, bl) + dot(al, bh))


def _inproj_kernel(x_ref, nw_ref, w_ref, main_ref, k_ref, v_ref, h_scr, *, n_main, n_kv, precise):
    j = pl.program_id(1)

    @pl.when(j == 0)
    def _():
        x = x_ref[...]
        ms = jnp.mean(x * x, axis=-1, keepdims=True)
        h_scr[...] = (x * lax.rsqrt(ms + EPS) * nw_ref[...]).astype(h_scr.dtype)

    acc = _mm(h_scr[...], w_ref[...], precise=precise)

    @pl.when(j < n_main)
    def _():
        main_ref[...] = acc

    @pl.when((j >= n_main) & (j < n_main + n_kv))
    def _():
        k_ref[...] = acc

    @pl.when(j >= n_main + n_kv)
    def _():
        v_ref[...] = acc


def _inproj(x, norm_w, w_in):
    n = x.shape[0]
    precise = w_in.dtype == F32
    tm = min(n, 1024)
    tn = 512
    n_main = MAIN_W // tn
    n_kv = GROUP_W // tn
    grid = (n // tm, n_main + 2 * n_kv)
    return pl.pallas_call(
        functools.partial(_inproj_kernel, n_main=n_main, n_kv=n_kv, precise=precise),
        grid=grid,
        in_specs=[
            pl.BlockSpec((tm, D_MODEL), lambda i, j: (i, 0)),
            pl.BlockSpec((1, D_MODEL), lambda i, j: (0, 0)),
            pl.BlockSpec((D_MODEL, tn), lambda i, j: (0, j)),
        ],
        out_specs=[
            pl.BlockSpec((tm, tn), lambda i, j: (i, jnp.minimum(j, n_main - 1))),
            pl.BlockSpec((tm, tn), lambda i, j: (i, jnp.clip(j - n_main, 0, n_kv - 1))),
            pl.BlockSpec((tm, tn), lambda i, j: (i, jnp.clip(j - n_main - n_kv, 0, n_kv - 1))),
        ],
        out_shape=[
            jax.ShapeDtypeStruct((n, MAIN_W), F32),
            jax.ShapeDtypeStruct((n, GROUP_W), F32),
            jax.ShapeDtypeStruct((n, GROUP_W), F32),
        ],
        scratch_shapes=[pltpu.VMEM((tm, D_MODEL), w_in.dtype)],
        compiler_params=_params("arbitrary", "arbitrary"),
        name="inproj",
    )(x, norm_w, w_in)


def _head(x, h):
    return x[:, h * HEAD_W:(h + 1) * HEAD_W]


def _hgrn_chunk(hq, hf, hi, lbv, states, a_scr, *, chunk, sub, t_valid, precise):
    mm = functools.partial(_mm, precise=precise)
    cast = (lambda x: x) if precise else (lambda x: x.astype(BF16))
    g = jnp.log(lbv + (1.0 - lbv) * jax.nn.sigmoid(hf))
    k = (1.0 - lbv) * jax.nn.sigmoid(-hf)
    q = hq * jax.nn.sigmoid(hq)
    if t_valid < chunk:
        live = lax.broadcasted_iota(jnp.int32, g.shape, 0) < t_valid
        g = jnp.where(live, g, 0.0)
        k = jnp.where(live, k, 0.0)
    row = lax.broadcasted_iota(jnp.int32, (chunk, chunk), 0)
    col = lax.broadcasted_iota(jnp.int32, (chunk, chunk), 1)
    tri = (col <= row).astype(BF16)
    g1 = g.astype(BF16)
    r1 = g - g1.astype(F32)
    g2 = r1.astype(BF16)
    g3 = (r1 - g2.astype(F32)).astype(BF16)
    tdot = lambda x: jnp.dot(tri, x, preferred_element_type=F32)
    b = tdot(g1) + (tdot(g2) + tdot(g3))
    vb = cast(hi)
    krow = lax.broadcasted_iota(jnp.int32, (chunk, GROUP_W), 0)
    for blk in range(chunk // sub):
        r0, r1 = blk * sub, (blk + 1) * sub
        beta = b[r0 - 1:r0] if blk > 0 else jnp.zeros((1, GROUP_W), F32)
        qt = cast(q[r0:r1] * jnp.exp(b[r0:r1] - beta))
        kt = cast(k * jnp.exp(jnp.where(krow < r1, beta - b, -jnp.inf)))
        trow = lax.broadcasted_iota(jnp.int32, (sub, chunk), 0) + r0
        scol = lax.broadcasted_iota(jnp.int32, (sub, chunk), 1)
        for h in range(HEADS):
            a_scr[h, r0:r1, :] = jnp.where(scol <= trow, mm(_head(qt, h), _head(kt, h), _NT), 0.0)
    qh = cast(q * jnp.exp(b))
    b_last = b[chunk - 1:chunk]
    kh = cast(k * jnp.exp(b_last - b))
    decay = jnp.exp(b_last)
    outs, new_states = [], []
    for h in range(HEADS):
        outs.append(mm(a_scr[h], _head(vb, h)) + mm(_head(qh, h), states[h], _NT))
        new_states.append(states[h] * _head(decay, h) + mm(_head(vb, h), _head(kh, h), _TN))
    return outs, new_states


def _hgrn_kernel(*refs, chunk, sub, t_valid, has_s0, precise):
    st_scr = refs[-HEADS:]
    a_scr = refs[-HEADS - 1]
    refs = refs[:-HEADS - 1]
    if has_s0:
        q_ref, f_ref, i_ref, g_ref, lb_ref, nw_ref, s0_ref, o_ref, s_ref = refs
    else:
        q_ref, f_ref, i_ref, g_ref, lb_ref, nw_ref, o_ref, s_ref = refs
        s0_ref = None
    c = pl.program_id(1)

    @pl.when(c == 0)
    def _():
        for h in range(HEADS):
            st_scr[h][...] = s0_ref[0, h].T if has_s0 else jnp.zeros((HEAD_W, HEAD_W), F32)

    outs, new_states = _hgrn_chunk(q_ref[...], f_ref[...], i_ref[...], lb_ref[...], [r[...] for r in st_scr],
                                   a_scr, chunk=chunk, sub=sub, t_valid=t_valid, precise=precise)
    normed = []
    for h in range(HEADS):
        st_scr[h][...] = new_states[h]
        o = outs[h]
        normed.append(o * lax.rsqrt(jnp.mean(o * o, axis=-1, keepdims=True) + EPS))
    hg = g_ref[...]
    o_all = jnp.concatenate(normed, axis=1) * nw_ref[...] * (hg * jax.nn.sigmoid(hg))
    o_ref[...] = o_all.astype(o_ref.dtype)

    @pl.when(c == pl.num_programs(1) - 1)
    def _():
        for h in range(HEADS):
            s_ref[0, h] = st_scr[h][...].T


def _hgrn(main, lbv, norm_w, s0, *, batch, chunk, sub, t_valid, precise):
    n = main.shape[0]
    n_chunks = n // batch // chunk
    slab = lambda s: pl.BlockSpec((chunk, GROUP_W), lambda b, c, s=s: (b * n_chunks + c, s))
    vec = pl.BlockSpec((1, GROUP_W), lambda b, c: (0, 0))
    state = pl.BlockSpec((1, HEADS, HEAD_W, HEAD_W), lambda b, c: (b, 0, 0, 0))
    in_specs = [slab(0), slab(1), slab(2), slab(3), vec, vec]
    args = [main, main, main, main, lbv, norm_w]
    if s0 is not None:
        in_specs.append(state)
        args.append(s0)
    return pl.pallas_call(
        functools.partial(_hgrn_kernel, chunk=chunk, sub=sub, t_valid=t_valid, has_s0=s0 is not None,
                          precise=precise),
        grid=(batch, n_chunks),
        in_specs=in_specs,
        out_specs=[pl.BlockSpec((chunk, GROUP_W), lambda b, c: (b * n_chunks + c, 0)), state],
        out_shape=[jax.ShapeDtypeStruct((n, GROUP_W), F32 if precise else BF16),
                   jax.ShapeDtypeStruct((batch, HEADS, HEAD_W, HEAD_W), F32)],
        scratch_shapes=([pltpu.VMEM((HEADS, chunk, chunk), F32)]
                        + [pltpu.VMEM((HEAD_W, HEAD_W), F32) for _ in range(HEADS)]),
        compiler_params=_params("arbitrary", "arbitrary"),
        name="hgrn",
    )(*args)


def _top_blocks(gate, n_sel):
    nb = gate.shape[0]
    blk = lax.broadcasted_iota(jnp.int32, gate.shape, 0)
    sel = jnp.zeros(gate.shape, F32)
    for _ in range(n_sel):
        mx = jnp.max(gate, axis=0, keepdims=True)
        first = jnp.min(jnp.where(gate == mx, blk, nb), axis=0, keepdims=True)
        hit = blk == first
        sel = jnp.where(hit & (mx > -jnp.inf), 1.0, sel)
        gate = jnp.where(hit, -jnp.inf, gate)
    return sel


def _moba_prompt_kernel(q_ref, k_ref, v_ref, o_ref, kb_scr, vt_scr, kmean_scr, sel_scr,
                        sa_scr, sb_scr, acc_scr, pa_scr, pb_scr, *, nb, n_sel):
    i = pl.program_id(2)
    blk = MOBA_BLOCK
    scale = HEAD_W ** -0.5

    @pl.when(i == 0)
    def _():
        def prep(n, carry):
            rows = pl.ds(pl.multiple_of(n * blk, blk), blk)
            kblk = k_ref[rows, :]
            kb_scr[n] = kblk.astype(BF16)
            kmean_scr[pl.ds(n, 1), :] = jnp.mean(kblk, axis=0, keepdims=True)
            vt_scr[n] = v_ref[rows, :].T.astype(BF16)
            return carry
        lax.fori_loop(0, nb, prep, 0)

    q = q_ref[...]
    qb = q.astype(BF16)
    if n_sel > 0:
        gate = lax.dot_general(kmean_scr[...], q, _NT, precision=HIGHEST, preferred_element_type=F32)
        bid = lax.broadcasted_iota(jnp.int32, gate.shape, 0)
        gate = jnp.where(bid < i, gate, -jnp.inf)
        sel_scr[...] = _top_blocks(gate, n_sel)

    c = scale * 1.4426950408889634

    def scores(n):
        return lax.dot_general(kb_scr[n], qb, _NT, preferred_element_type=F32)

    s = scores(i)
    kpos = lax.broadcasted_iota(jnp.int32, s.shape, 0)
    qpos = lax.broadcasted_iota(jnp.int32, s.shape, 1)
    s = jnp.where(kpos <= qpos, s, NEG_BIG)
    m = jnp.max(s, axis=0, keepdims=True)
    p = jnp.exp2((s - m) * c)
    l = jnp.sum(p, axis=0, keepdims=True)

    acc_scr[...] = jnp.zeros_like(acc_scr)
    pa_scr[...] = jnp.zeros_like(pa_scr)
    pb_scr[...] = p.astype(BF16)

    def pv(n, p_ref):
        return jnp.dot(vt_scr[n], p_ref[...], preferred_element_type=F32)

    def softmax_step(s_ref, p_ref, n, valid, m, l):
        picked = (sel_scr[pl.ds(n, 1), :] > 0.5) & valid
        s = s_ref[...]
        m_new = jnp.where(picked, jnp.maximum(m, jnp.max(s, axis=0, keepdims=True)), m)
        p = jnp.exp2((s - jnp.where(picked, m_new, -NEG_BIG)) * c)
        l = jnp.exp2((m - m_new) * c) * l + jnp.sum(p, axis=0, keepdims=True)
        p_ref[...] = p.astype(BF16)
        return m_new, l

    def body(j, carry):
        m_in, l, ma_prev, na_prev, nb_prev = carry
        n_a = 2 * j
        n_b = jnp.minimum(n_a + 1, nb - 1)
        sb_scr[...] = scores(n_b)
        pva = pv(na_prev, pa_scr)
        pvb = pv(nb_prev, pb_scr)
        m_a, l = softmax_step(sa_scr, pa_scr, n_a, True, m_in, l)
        sa_scr[...] = scores(jnp.minimum(n_a + 2, nb - 1))
        m_b, l = softmax_step(sb_scr, pb_scr, n_b, n_a + 1 < i, m_a, l)
        acc_scr[...] = (jnp.exp2((m_in - m_b) * c) * (acc_scr[...] + pvb)
                        + jnp.exp2((ma_prev - m_b) * c) * pva)
        return m_b, l, m_a, n_a, n_b

    carry = (m, l, m, jnp.int32(0), i)
    if n_sel > 0:
        @pl.when(i > 0)
        def _():
            sa_scr[...] = scores(0)
        carry = lax.fori_loop(0, (i + 1) // 2, body, carry)
    m_f, l, ma_last, na_last, nb_last = carry
    acc = acc_scr[...] + pv(nb_last, pb_scr) + jnp.exp2((ma_last - m_f) * c) * pv(na_last, pa_scr)
    o_ref[...] = (acc / l).T.astype(BF16)


def _moba_prompt(main, k, v, *, batch):
    n = main.shape[0]
    seq = n // batch
    nb = seq // MOBA_BLOCK
    n_sel = min(MOBA_TOPK, nb - 1)
    q_col0 = 4 * HEADS
    return pl.pallas_call(
        functools.partial(_moba_prompt_kernel, nb=nb, n_sel=n_sel),
        grid=(batch, HEADS, nb),
        in_specs=[
            pl.BlockSpec((MOBA_BLOCK, HEAD_W), lambda b, h, i: (b * nb + i, q_col0 + h)),
            pl.BlockSpec((seq, HEAD_W), lambda b, h, i: (b, h)),
            pl.BlockSpec((seq, HEAD_W), lambda b, h, i: (b, h)),
        ],
        out_specs=pl.BlockSpec((MOBA_BLOCK, HEAD_W), lambda b, h, i: (b * nb + i, h)),
        out_shape=jax.ShapeDtypeStruct((n, GROUP_W), BF16),
        scratch_shapes=[
            pltpu.VMEM((nb, MOBA_BLOCK, HEAD_W), BF16),
            pltpu.VMEM((nb, HEAD_W, MOBA_BLOCK), BF16),
            pltpu.VMEM((max(nb, 8), HEAD_W), F32),
            pltpu.VMEM((max(nb, 8), MOBA_BLOCK), F32),
            pltpu.VMEM((MOBA_BLOCK, MOBA_BLOCK), F32),
            pltpu.VMEM((MOBA_BLOCK, MOBA_BLOCK), F32),
            pltpu.VMEM((HEAD_W, MOBA_BLOCK), F32),
            pltpu.VMEM((MOBA_BLOCK, MOBA_BLOCK), BF16),
            pltpu.VMEM((MOBA_BLOCK, MOBA_BLOCK), BF16),
        ],
        compiler_params=_params("arbitrary", "arbitrary", "arbitrary"),
        name="moba_prompt",
    )(main, k, v)


def _route(logits):
    lane = lax.broadcasted_iota(jnp.int32, logits.shape, 1)
    is_g = lane < N_GROUPS
    gl = jnp.where(is_g, logits, -jnp.inf)
    gmax = jnp.max(gl, axis=-1, keepdims=True)
    g_sel = jnp.min(jnp.where(gl == gmax, lane, LANES), axis=-1, keepdims=True)
    p_group = 1.0 / jnp.sum(jnp.where(is_g, jnp.exp(gl - gmax), 0.0), axis=-1, keepdims=True)
    e_lo = N_GROUPS + EXPERTS_PER_GROUP * g_sel
    is_e = (lane >= e_lo) & (lane < e_lo + EXPERTS_PER_GROUP)
    el = jnp.where(is_e, logits, -jnp.inf)
    emax = jnp.max(el, axis=-1, keepdims=True)
    ee = jnp.where(is_e, jnp.exp(el - emax), 0.0)
    prob = ee / jnp.sum(ee, axis=-1, keepdims=True)
    prob = jnp.where(is_e, prob, -1.0)
    p1 = jnp.max(prob, axis=-1, keepdims=True)
    i1 = jnp.min(jnp.where(prob == p1, lane, LANES), axis=-1, keepdims=True)
    prob2 = jnp.where(lane == i1, -1.0, prob)
    p2 = jnp.max(prob2, axis=-1, keepdims=True)
    i2 = jnp.min(jnp.where(prob2 == p2, lane, LANES), axis=-1, keepdims=True)
    tot = p1 + p2
    w1 = p_group * (p1 / tot)
    w2 = p_group * (p2 / tot)
    e1 = (i1 - N_GROUPS).astype(F32)
    e2 = (i2 - N_GROUPS).astype(F32)
    return jnp.where(lane == 0, e1, jnp.where(lane == 1, e2, jnp.where(lane == 2, w1, jnp.where(lane == 3, w2, 0.0))))


def _outproj_kernel(x_ref, oh_ref, oa_ref, w_ref, nw_ref, wr_ref, x1_ref, hn_ref, route_ref, *, precise):
    y = _mm(oh_ref[...], w_ref[0:GROUP_W, :], precise=precise)
    y = y + _mm(oa_ref[...], w_ref[GROUP_W:2 * GROUP_W, :], precise=precise)
    x1 = x_ref[...] + y
    x1_ref[...] = x1
    ms = jnp.mean(x1 * x1, axis=-1, keepdims=True)
    hn = x1 * lax.rsqrt(ms + EPS) * nw_ref[...]
    hn_ref[...] = hn
    if precise:
        logits = jnp.dot(hn, wr_ref[...], precision=HIGHEST, preferred_element_type=F32)
    else:
        logits = _mm3(hn, wr_ref[...])
    route_ref[...] = _route(logits)


def _outproj(x, oh, oa, w_out, norm_w, w_router):
    n = x.shape[0]
    tm = min(n, 512)
    row = lambda w: pl.BlockSpec((tm, w), lambda i: (i, 0))
    full = lambda a, b: pl.BlockSpec((a, b), lambda i: (0, 0), pipeline_mode=pl.Buffered(1))
    return pl.pallas_call(
        functools.partial(_outproj_kernel, precise=w_out.dtype == F32),
        grid=(n // tm,),
        in_specs=[row(D_MODEL), row(GROUP_W), row(GROUP_W), full(2 * GROUP_W, D_MODEL), full(1, D_MODEL),
                  full(D_MODEL, LANES)],
        out_specs=[row(D_MODEL), row(D_MODEL), row(LANES)],
        out_shape=[jax.ShapeDtypeStruct((n, D_MODEL), F32), jax.ShapeDtypeStruct((n, D_MODEL), F32),
                   jax.ShapeDtypeStruct((n, LANES), F32)],
        compiler_params=_params("arbitrary"),
        name="outproj",
    )(x, oh, oa, w_out, norm_w, w_router)


def _moe_kernel(te_ref, tv_ref, rt_ref, hn_hbm, rw_ref, wg_ref, wu_ref, wd_ref, y_ref, xbuf, sem):
    t = pl.program_id(0)
    nt = pl.num_programs(0)
    slot = t % 2

    def row_copy(tile, r, slot):
        tok = rt_ref[tile * MOE_TILE + r]
        return pltpu.make_async_copy(hn_hbm.at[pl.ds(tok, 1)], xbuf.at[slot, pl.ds(r, 1)], sem.at[slot])

    def issue(tile, slot):
        def body(r, carry):
            row_copy(tile, r, slot).start()
            return carry
        lax.fori_loop(0, MOE_TILE, body, 0, unroll=8)

    @pl.when(t == 0)
    def _():
        issue(0, 0)

    @pl.when(t + 1 < nt)
    def _():
        issue(t + 1, 1 - slot)

    def wait_body(r, carry):
        row_copy(t, r, slot).wait()
        return carry
    lax.fori_loop(0, MOE_TILE, wait_body, 0, unroll=8)

    @pl.when(tv_ref[t] > 0)
    def _():
        x = xbuf[slot].astype(BF16)
        a = jnp.dot(x, wg_ref[0], preferred_element_type=F32)
        b = jnp.dot(x, wu_ref[0], preferred_element_type=F32)
        rw = rw_ref[...]
        act = (a * jax.nn.sigmoid(a)) * b * jnp.concatenate([rw] * (EXPERT_FF // LANES), axis=1)
        y_ref[...] = jnp.dot(act.astype(BF16), wd_ref[0], preferred_element_type=F32)

    @pl.when(tv_ref[t] == 0)
    def _():
        y_ref[...] = jnp.zeros_like(y_ref)


def _moe(hn, tile_expert, tile_valid, row_token, row_w, wg, wu, wd):
    n_rows = row_token.shape[0]
    n_tiles = n_rows // MOE_TILE
    grid_spec = pltpu.PrefetchScalarGridSpec(
        num_scalar_prefetch=3,
        grid=(n_tiles,),
        in_specs=[
            pl.BlockSpec(memory_space=pl.ANY),
            pl.BlockSpec((MOE_TILE, LANES), lambda t, te, tv, rt: (t, 0)),
            pl.BlockSpec((1, D_MODEL, EXPERT_FF), lambda t, te, tv, rt: (te[t], 0, 0)),
            pl.BlockSpec((1, D_MODEL, EXPERT_FF), lambda t, te, tv, rt: (te[t], 0, 0)),
            pl.BlockSpec((1, EXPERT_FF, D_MODEL), lambda t, te, tv, rt: (te[t], 0, 0)),
        ],
        out_specs=pl.BlockSpec((MOE_TILE, D_MODEL), lambda t, te, tv, rt: (t, 0)),
        scratch_shapes=[pltpu.VMEM((2, MOE_TILE, D_MODEL), F32), pltpu.SemaphoreType.DMA((2,))],
    )
    return pl.pallas_call(
        _moe_kernel,
        grid_spec=grid_spec,
        out_shape=jax.ShapeDtypeStruct((n_rows, D_MODEL), F32),
        compiler_params=_params("arbitrary"),
        name="moe",
    )(tile_expert, tile_valid, row_token, hn, row_w, wg, wu, wd)


def _moe_plan(route, n_rows):
    n = route.shape[0]
    ids = route[:, 0:2].astype(jnp.int32).reshape(-1)
    wts = route[:, 2:4].reshape(-1)
    onehot = (ids[:, None] == jnp.arange(N_EXPERTS, dtype=jnp.int32)[None, :]).astype(jnp.int32)
    csum = jnp.cumsum(onehot, axis=0)
    rank = jnp.take_along_axis(csum, ids[:, None], axis=1)[:, 0] - 1
    counts = csum[-1]
    padded = ((counts + MOE_TILE - 1) // MOE_TILE) * MOE_TILE
    ends = jnp.cumsum(padded)
    starts = ends - padded
    dest = starts[ids] + rank
    token = jnp.arange(2 * n, dtype=jnp.int32) // 2
    row_token = jnp.zeros((n_rows,), jnp.int32).at[dest].set(token)
    row_w = jnp.zeros((n_rows,), F32).at[dest].set(wts)
    row_w = jnp.broadcast_to(row_w[:, None], (n_rows, LANES))
    tile_start = jnp.arange(n_rows // MOE_TILE, dtype=jnp.int32) * MOE_TILE
    tile_expert = jnp.sum((tile_start[:, None] >= ends[None, :]).astype(jnp.int32), axis=1)
    tile_valid = (tile_expert < N_EXPERTS).astype(jnp.int32)
    last = jnp.max(jnp.where(counts > 0, jnp.arange(N_EXPERTS, dtype=jnp.int32), 0))
    tile_expert = jnp.where(tile_valid > 0, tile_expert, last).astype(jnp.int32)
    return tile_expert, tile_valid, row_token, row_w, dest


def _combine_kernel(dest_ref, x1_ref, y_hbm, nw_ref, o_ref, ybuf, sem, *, tm):
    t = pl.program_id(0)
    nt = pl.num_programs(0)
    slot = t % 2

    def row_copy(tile, r, slot):
        src = dest_ref[tile * (2 * tm) + r]
        return pltpu.make_async_copy(y_hbm.at[pl.ds(src, 1)], ybuf.at[slot, pl.ds(r, 1)], sem.at[slot])

    def issue(tile, slot):
        def body(r, carry):
            row_copy(tile, r, slot).start()
            return carry
        lax.fori_loop(0, 2 * tm, body, 0, unroll=8)

    @pl.when(t == 0)
    def _():
        issue(0, 0)

    @pl.when(t + 1 < nt)
    def _():
        issue(t + 1, 1 - slot)

    def wait_body(r, carry):
        row_copy(t, r, slot).wait()
        return carry
    lax.fori_loop(0, 2 * tm, wait_body, 0, unroll=8)

    x = x1_ref[...] + ybuf[slot, 0:tm, :] + ybuf[slot, tm:2 * tm, :]
    ms = jnp.mean(x * x, axis=-1, keepdims=True)
    o_ref[...] = x * lax.rsqrt(ms + EPS) * nw_ref[...]


def _combine(x1, y_sorted, dest_tiled, norm_w):
    n = x1.shape[0]
    tm = min(n, 128)
    grid_spec = pltpu.PrefetchScalarGridSpec(
        num_scalar_prefetch=1,
        grid=(n // tm,),
        in_specs=[
            pl.BlockSpec((tm, D_MODEL), lambda t, d: (t, 0)),
            pl.BlockSpec(memory_space=pl.ANY),
            pl.BlockSpec((1, D_MODEL), lambda t, d: (0, 0)),
        ],
        out_specs=pl.BlockSpec((tm, D_MODEL), lambda t, d: (t, 0)),
        scratch_shapes=[pltpu.VMEM((2, 2 * tm, D_MODEL), F32), pltpu.SemaphoreType.DMA((2,))],
    )
    return pl.pallas_call(
        functools.partial(_combine_kernel, tm=tm),
        grid_spec=grid_spec,
        out_shape=jax.ShapeDtypeStruct((n, D_MODEL), F32),
        compiler_params=_params("arbitrary"),
        name="combine",
    )(dest_tiled, x1, y_sorted, norm_w)


def _ffn_and_final(x1, hn, route, wg, wu, wd, norm_final):
    n = x1.shape[0]
    n_rows = 2 * n + N_EXPERTS * MOE_TILE
    tile_expert, tile_valid, row_token, row_w, dest = _moe_plan(route, n_rows)
    y_sorted = _moe(hn, tile_expert, tile_valid, row_token, row_w, wg, wu, wd)
    tm = min(n, 128)
    dest_tiled = dest.reshape(n // tm, tm, 2).transpose(0, 2, 1).reshape(-1)
    return _combine(x1, y_sorted, dest_tiled, norm_final)


def _kmean_kernel(pt_ref, *refs, pages_per_step):
    page_refs = refs[:pages_per_step]
    o_ref = refs[pages_per_step]
    j = pl.program_id(1)
    ppb = MOBA_BLOCK // PAGE_SIZE
    for blk in range(pages_per_step // ppb):
        tot = jnp.sum(page_refs[blk * ppb][0], axis=0)
        for pg in range(1, ppb):
            tot = tot + jnp.sum(page_refs[blk * ppb + pg][0], axis=0)
        o_ref[0, j * (pages_per_step // ppb) + blk] = tot * (1.0 / MOBA_BLOCK)


def _kmean(cache_k, page_table):
    dec_batch, n_pages = page_table.shape
    ppb = MOBA_BLOCK // PAGE_SIZE
    n_blocks = n_pages // ppb
    pages_per_step = 8
    steps = n_pages // pages_per_step

    def page_spec(p):
        return pl.BlockSpec((1, PAGE_SIZE, HEADS, HEAD_W),
                            lambda s, j, pt, p=p: (pt[s * n_pages + j * pages_per_step + p], 0, 0, 0))

    grid_spec = pltpu.PrefetchScalarGridSpec(
        num_scalar_prefetch=1,
        grid=(dec_batch, steps),
        in_specs=[page_spec(p) for p in range(pages_per_step)],
        out_specs=pl.BlockSpec((1, n_blocks, HEADS, HEAD_W), lambda s, j, pt: (s, 0, 0, 0)),
    )
    return pl.pallas_call(
        functools.partial(_kmean_kernel, pages_per_step=pages_per_step),
        grid_spec=grid_spec,
        out_shape=jax.ShapeDtypeStruct((dec_batch, n_blocks, HEADS, HEAD_W), F32),
        compiler_params=_params("arbitrary", "arbitrary"),
        name="kmean",
    )(page_table.reshape(-1), *([cache_k] * pages_per_step))


def _sample_topk_kernel(q_ref, kmean_ref, idx_ref, q_scr, *, t_new, n_sel):
    q_scr[...] = jnp.zeros_like(q_scr)
    for h in range(HEADS):
        sl = slice(h * HEAD_W, (h + 1) * HEAD_W)
        q_scr[0:t_new, :] = q_ref[0, :, sl]
        gate = lax.dot_general(kmean_ref[0, :, h, :], q_scr[...], _NT, precision=HIGHEST,
                               preferred_element_type=F32)
        nb = gate.shape[0]
        blk = lax.broadcasted_iota(jnp.int32, gate.shape, 0)
        row = lax.broadcasted_iota(jnp.int32, (8, LANES), 0)
        tile = jnp.zeros((8, LANES), jnp.int32)
        for r in range(n_sel):
            mx = jnp.max(gate, axis=0, keepdims=True)
            first = jnp.min(jnp.where(gate == mx, blk, nb), axis=0, keepdims=True)
            tile = jnp.where(row == r, jnp.broadcast_to(first, (8, LANES)), tile)
            gate = jnp.where(blk == first, -jnp.inf, gate)
        idx_ref[0, h] = tile


def _sample_topk(main3, kmean, *, n_sel):
    dec_batch, t_new, _ = main3.shape
    n_blocks = kmean.shape[1]
    return pl.pallas_call(
        functools.partial(_sample_topk_kernel, t_new=t_new, n_sel=n_sel),
        grid=(dec_batch,),
        in_specs=[
            pl.BlockSpec((1, t_new, GROUP_W), lambda s: (s, 0, 4)),
            pl.BlockSpec((1, n_blocks, HEADS, HEAD_W), lambda s: (s, 0, 0, 0)),
        ],
        out_specs=pl.BlockSpec((1, HEADS, 8, LANES), lambda s: (s, 0, 0, 0)),
        out_shape=jax.ShapeDtypeStruct((dec_batch, HEADS, 8, LANES), jnp.int32),
        scratch_shapes=[pltpu.VMEM((LANES, HEAD_W), F32)],
        compiler_params=_params("arbitrary"),
        name="sample_topk",
    )(main3, kmean)


def _sample_attn_kernel(idx_ref, pt_ref, q_ref, kn_ref, vn_ref, ck_hbm, cv_hbm, o_ref,
                        kbuf, vbuf, q_scr, kn_scr, vn_scr, sem, *, t_new, n_sel, n_pages):
    s = pl.program_id(0)
    h = pl.program_id(1)
    step = s * HEADS + h
    n_steps = pl.num_programs(0) * HEADS
    slot = step % 2
    ppb = MOBA_BLOCK // PAGE_SIZE
    scale = HEAD_W ** -0.5

    def copies(step, slot):
        s_ = step // HEADS
        h_ = step % HEADS
        out = []
        for t in range(t_new):
            for r in range(n_sel):
                b = idx_ref[(step * t_new + t) * n_sel + r]
                for pg in range(ppb):
                    page = pt_ref[s_ * n_pages + b * ppb + pg]
                    rows = pl.ds((r * ppb + pg) * PAGE_SIZE, PAGE_SIZE)
                    out.append(pltpu.make_async_copy(ck_hbm.at[page, :, h_, :], kbuf.at[slot, t, rows, :],
                                                     sem.at[slot]))
                    out.append(pltpu.make_async_copy(cv_hbm.at[page, :, h_, :], vbuf.at[slot, t, rows, :],
                                                     sem.at[slot]))
        return out

    @pl.when(step == 0)
    def _():
        for c in copies(0, 0):
            c.start()

    @pl.when(step + 1 < n_steps)
    def _():
        for c in copies(step + 1, 1 - slot):
            c.start()

    q_scr[...] = jnp.zeros_like(q_scr)
    kn_scr[...] = jnp.zeros_like(kn_scr)
    vn_scr[...] = jnp.zeros_like(vn_scr)
    q_scr[0:t_new, :] = q_ref[0]
    kn_scr[0:t_new, :] = kn_ref[0]
    vn_scr[0:t_new, :] = vn_ref[0]
    mm = _mm3
    qb = q_scr[...]
    s_new = mm(qb, kn_scr[...], _NT) * scale
    trow = lax.broadcasted_iota(jnp.int32, s_new.shape, 0)
    jcol = lax.broadcasted_iota(jnp.int32, s_new.shape, 1)
    new_ok = (jcol <= trow) & (jcol < t_new)
    s_new = jnp.where(new_ok, s_new, NEG_BIG)
    m_new = jnp.max(s_new, axis=-1, keepdims=True)

    for c in copies(step, slot):
        c.wait()

    n_keys = n_sel * MOBA_BLOCK
    st = mm(qb, kbuf[slot].reshape(t_new * n_keys, HEAD_W), _NT) * scale
    srow = lax.broadcasted_iota(jnp.int32, st.shape, 0)
    scol = lax.broadcasted_iota(jnp.int32, st.shape, 1)
    own = (scol >= srow * n_keys) & (scol < (srow + 1) * n_keys)
    st = jnp.where(own, st, NEG_BIG)
    m = jnp.maximum(jnp.max(st, axis=-1, keepdims=True), m_new)
    p = jnp.exp(st - m)
    pn = jnp.where(new_ok, jnp.exp(s_new - m), 0.0)
    den = jnp.sum(p, axis=-1, keepdims=True) + jnp.sum(pn, axis=-1, keepdims=True)
    out = (mm(p, vbuf[slot].reshape(t_new * n_keys, HEAD_W)) + mm(pn, vn_scr[...])) / den
    o_ref[0] = out[0:t_new, :]


def _sample_attn(idx, page_table, main3, kn3, vn3, cache_k4, cache_v4, *, n_sel):
    dec_batch, t_new, _ = main3.shape
    n_pages = page_table.shape[1]
    q_col0 = 4 * HEADS
    grid_spec = pltpu.PrefetchScalarGridSpec(
        num_scalar_prefetch=2,
        grid=(dec_batch, HEADS),
        in_specs=[
            pl.BlockSpec((1, t_new, HEAD_W), lambda s, h, ix, pt: (s, 0, q_col0 + h)),
            pl.BlockSpec((1, t_new, HEAD_W), lambda s, h, ix, pt: (s, 0, h)),
            pl.BlockSpec((1, t_new, HEAD_W), lambda s, h, ix, pt: (s, 0, h)),
            pl.BlockSpec(memory_space=pl.ANY),
            pl.BlockSpec(memory_space=pl.ANY),
        ],
        out_specs=pl.BlockSpec((1, t_new, HEAD_W), lambda s, h, ix, pt: (s, 0, h)),
        scratch_shapes=[
            pltpu.VMEM((2, t_new, n_sel * MOBA_BLOCK, HEAD_W), F32),
            pltpu.VMEM((2, t_new, n_sel * MOBA_BLOCK, HEAD_W), F32),
            pltpu.VMEM((8, HEAD_W), F32),
            pltpu.VMEM((16, HEAD_W), F32),
            pltpu.VMEM((16, HEAD_W), F32),
            pltpu.SemaphoreType.DMA((2,)),
        ],
    )
    return pl.pallas_call(
        functools.partial(_sample_attn_kernel, t_new=t_new, n_sel=n_sel, n_pages=n_pages),
        grid_spec=grid_spec,
        out_shape=jax.ShapeDtypeStruct((dec_batch, t_new, GROUP_W), F32),
        compiler_params=_params("arbitrary", "arbitrary"),
        name="sample_attn",
    )(idx, page_table.reshape(-1), main3, kn3, vn3, cache_k4, cache_v4)


def _moba_sample(main, kn, vn, cache_k, cache_v, page_table):
    dec_batch, n_pages = page_table.shape
    t_new = main.shape[0] // dec_batch
    n_blocks = n_pages * PAGE_SIZE // MOBA_BLOCK
    n_sel = min(MOBA_TOPK, n_blocks)
    main3 = main.reshape(dec_batch, t_new, MAIN_W)
    kn3 = kn.reshape(dec_batch, t_new, GROUP_W)
    vn3 = vn.reshape(dec_batch, t_new, GROUP_W)
    kmean = _kmean(cache_k, page_table)
    idx = _sample_topk(main3, kmean, n_sel=n_sel)
    idx = idx[:, :, :n_sel, :t_new].transpose(0, 1, 3, 2).reshape(-1)
    o = _sample_attn(idx, page_table, main3, kn3, vn3, cache_k, cache_v, n_sel=n_sel)
    return o.reshape(dec_batch * t_new, GROUP_W)


def _layer(x, attend, hgrn_fn, weights):
    norm_mix, w_in, lbv, hgrn_norm, w_out, norm_ffn, w_router, wg, wu, wd, norm_final = weights
    main, k, v = _inproj(x, norm_mix, w_in)
    oh, state = hgrn_fn(main, lbv, hgrn_norm)
    oa = attend(main, k, v)
    x1, hn, route = _outproj(x, oh, oa, w_out, norm_ffn, w_router)
    y = _ffn_and_final(x1, hn, route, wg, wu, wd, norm_final)
    return y, k, v, state


def kernel(x_prompt, x_sample, cache_k, cache_v, state_hgrn, page_table, norm_mix, w_in, lb_logits, hgrn_norm,
           w_out, norm_ffn, w_group, w_expert, w_gate, w_up, w_down, norm_final):
    depth = w_in.shape[0]
    assert depth == 1, "the final norm is fused into the (single) layer"
    batch, seq, _ = x_prompt.shape
    dec_batch, t_new, _ = x_sample.shape
    lb_all = jnp.cumsum(jax.nn.softmax(lb_logits.astype(F32), axis=0), axis=0)
    layer = 0
    w_router = jnp.concatenate(
        [w_group[layer], w_expert[layer], jnp.zeros((D_MODEL, LANES - N_GROUPS - N_EXPERTS), F32)], axis=1)
    def weights(mix_dtype):
        return (
            norm_mix[layer][None, :], w_in[layer].astype(mix_dtype), lb_all[layer][None, :],
            hgrn_norm[layer][None, :], w_out[layer].astype(mix_dtype), norm_ffn[layer][None, :], w_router,
            w_gate[layer].reshape(N_EXPERTS, D_MODEL, EXPERT_FF).astype(BF16),
            w_up[layer].reshape(N_EXPERTS, D_MODEL, EXPERT_FF).astype(BF16),
            w_down[layer].reshape(N_EXPERTS, EXPERT_FF, D_MODEL).astype(BF16),
            norm_final[None, :],
        )

    hgrn_p = functools.partial(_hgrn, s0=None, batch=batch, chunk=HGRN_CHUNK, sub=HGRN_SUB, t_valid=HGRN_CHUNK,
                               precise=False)
    moba_p = functools.partial(_moba_prompt, batch=batch)
    yp, kp, vp, sp = _layer(x_prompt.reshape(batch * seq, D_MODEL), moba_p, hgrn_p, weights(BF16))

    pad = 8

    def hgrn_s(main, lbv, nw):
        m3 = main.reshape(dec_batch, t_new, MAIN_W)
        m3 = jnp.pad(m3, ((0, 0), (0, pad - t_new), (0, 0))).reshape(dec_batch * pad, MAIN_W)
        o, st = _hgrn(m3, lbv, nw, state_hgrn[layer], batch=dec_batch, chunk=pad, sub=pad, t_valid=t_new,
                      precise=True)
        return o.reshape(dec_batch, pad, GROUP_W)[:, :t_new].reshape(dec_batch * t_new, GROUP_W), st

    moba_s = functools.partial(_moba_sample, cache_k=cache_k.reshape(cache_k.shape[1:]),
                               cache_v=cache_v.reshape(cache_v.shape[1:]), page_table=page_table)
    ys, ks, vs, ss = _layer(x_sample.reshape(dec_batch * t_new, D_MODEL), moba_s, hgrn_s, weights(F32))

    return (yp.reshape(batch, seq, D_MODEL), ys.reshape(dec_batch, t_new, D_MODEL),
            kp.reshape(1, batch, seq, HEADS, HEAD_W), vp.reshape(1, batch, seq, HEADS, HEAD_W), sp[None],
            ks.reshape(1, dec_batch, t_new, HEADS, HEAD_W), vs.reshape(1, dec_batch, t_new, HEADS, HEAD_W), ss[None])
```

```python
import functools

import jax
import jax.numpy as jnp
from jax import lax
from jax.experimental import pallas as pl
from jax.experimental.pallas import tpu as pltpu

F32 = jnp.float32
BF16 = jnp.bfloat16
HIGHEST = lax.Precision.HIGHEST

D_MODEL = 2048
HEADS = 8
HEAD_W = 128
GROUP_W = HEADS * HEAD_W
MAIN_W = 5 * GROUP_W
HGRN_CHUNK = 64
HGRN_SUB = 16
MOBA_BLOCK = 256
MOBA_TOPK = 3
PAGE_SIZE = 128
N_GROUPS = 4
EXPERTS_PER_GROUP = 8
N_EXPERTS = N_GROUPS * EXPERTS_PER_GROUP
EXPERT_FF = 256
EPS = 1e-6
LANES = 128
MOE_TILE = 256
MOE_TILE_MIN = 32
VMEM_LIMIT = 52 * 1024 * 1024
NEG_BIG = -1e30

_NT = (((1,), (1,)), ((), ()))
_TN = (((0,), (0,)), ((), ()))


def _params(*sem):
    return pltpu.CompilerParams(dimension_semantics=sem, vmem_limit_bytes=VMEM_LIMIT)


def _mm(a, b, dims=None, *, precise):
    if dims is None:
        dims = (((a.ndim - 1,), (0,)), ((), ()))
    if precise:
        return lax.dot_general(a.astype(F32), b.astype(F32), dims, precision=HIGHEST, preferred_element_type=F32)
    return lax.dot_general(a.astype(BF16), b.astype(BF16), dims, preferred_element_type=F32)


def _mm3(a, b, dims=None):
    if dims is None:
        dims = (((a.ndim - 1,), (0,)), ((), ()))

    def halves(x):
        hi = x.astype(BF16)
        return hi, (x - hi.astype(F32)).astype(BF16)

    (ah, al), (bh, bl) = halves(a), halves(b)
    dot = lambda x, y: lax.dot_general(x, y, dims, preferred_element_type=F32)
    return dot(ah, bh) + (dot(ah, bl) + dot(al, bh))


def _inproj_kernel(x_ref, nw_ref, w_ref, main_ref, k_ref, v_ref, h_scr, *, n_main, n_kv, precise):
    j = pl.program_id(1)

    @pl.when(j == 0)
    def _():
        x = x_ref[...]
        ms = jnp.mean(x * x, axis=-1, keepdims=True)
        h_scr[...] = (x * lax.rsqrt(ms + EPS) * nw_ref[...]).astype(h_scr.dtype)

    acc = _mm(h_scr[...], w_ref[...], precise=precise)

    @pl.when(j < n_main)
    def _():
        main_ref[...] = acc

    @pl.when((j >= n_main) & (j < n_main + n_kv))
    def _():
        k_ref[...] = acc

    @pl.when(j >= n_main + n_kv)
    def _():
        v_ref[...] = acc


def _inproj(x, norm_w, w_in):
    n = x.shape[0]
    precise = w_in.dtype == F32
    tm = min(n, 1024)
    tn = 512
    n_main = MAIN_W // tn
    n_kv = GROUP_W // tn
    grid = (n // tm, n_main + 2 * n_kv)
    return pl.pallas_call(
        functools.partial(_inproj_kernel, n_main=n_main, n_kv=n_kv, precise=precise),
        grid=grid,
        in_specs=[
            pl.BlockSpec((tm, D_MODEL), lambda i, j: (i, 0)),
            pl.BlockSpec((1, D_MODEL), lambda i, j: (0, 0)),
            pl.BlockSpec((D_MODEL, tn), lambda i, j: (0, j)),
        ],
        out_specs=[
            pl.BlockSpec((tm, tn), lambda i, j: (i, jnp.minimum(j, n_main - 1))),
            pl.BlockSpec((tm, tn), lambda i, j: (i, jnp.clip(j - n_main, 0, n_kv - 1))),
            pl.BlockSpec((tm, tn), lambda i, j: (i, jnp.clip(j - n_main - n_kv, 0, n_kv - 1))),
        ],
        out_shape=[
            jax.ShapeDtypeStruct((n, MAIN_W), F32),
            jax.ShapeDtypeStruct((n, GROUP_W), F32),
            jax.ShapeDtypeStruct((n, GROUP_W), F32),
        ],
        scratch_shapes=[pltpu.VMEM((tm, D_MODEL), w_in.dtype)],
        compiler_params=_params("arbitrary", "arbitrary"),
        name="inproj",
    )(x, norm_w, w_in)


def _head(x, h):
    return x[:, h * HEAD_W:(h + 1) * HEAD_W]


def _hgrn_chunk(hq, hf, hi, lbv, states, a_scr, *, chunk, sub, t_valid, precise):
    mm = functools.partial(_mm, precise=precise)
    cast = (lambda x: x) if precise else (lambda x: x.astype(BF16))
    g = jnp.log(lbv + (1.0 - lbv) * jax.nn.sigmoid(hf))
    k = (1.0 - lbv) * jax.nn.sigmoid(-hf)
    q = hq * jax.nn.sigmoid(hq)
    if t_valid < chunk:
        live = lax.broadcasted_iota(jnp.int32, g.shape, 0) < t_valid
        g = jnp.where(live, g, 0.0)
        k = jnp.where(live, k, 0.0)
    row = lax.broadcasted_iota(jnp.int32, (chunk, chunk), 0)
    col = lax.broadcasted_iota(jnp.int32, (chunk, chunk), 1)
    tri = (col <= row).astype(BF16)
    g1 = g.astype(BF16)
    r1 = g - g1.astype(F32)
    g2 = r1.astype(BF16)
    g3 = (r1 - g2.astype(F32)).astype(BF16)
    tdot = lambda x: jnp.dot(tri, x, preferred_element_type=F32)
    b = tdot(g1) + (tdot(g2) + tdot(g3))
    vb = cast(hi)
    krow = lax.broadcasted_iota(jnp.int32, (chunk, GROUP_W), 0)
    for blk in range(chunk // sub):
        r0, r1 = blk * sub, (blk + 1) * sub
        beta = b[r0 - 1:r0] if blk > 0 else jnp.zeros((1, GROUP_W), F32)
        qt = cast(q[r0:r1] * jnp.exp(b[r0:r1] - beta))
        kt = cast(k * jnp.exp(jnp.where(krow < r1, beta - b, -jnp.inf)))
        trow = lax.broadcasted_iota(jnp.int32, (sub, chunk), 0) + r0
        scol = lax.broadcasted_iota(jnp.int32, (sub, chunk), 1)
        for h in range(HEADS):
            a_scr[h, r0:r1, :] = jnp.where(scol <= trow, mm(_head(qt, h), _head(kt, h), _NT), 0.0)
    qh = cast(q * jnp.exp(b))
    b_last = b[chunk - 1:chunk]
    kh = cast(k * jnp.exp(b_last - b))
    decay = jnp.exp(b_last)
    outs, new_states = [], []
    for h in range(HEADS):
        outs.append(mm(a_scr[h], _head(vb, h)) + mm(_head(qh, h), states[h], _NT))
        new_states.append(states[h] * _head(decay, h) + mm(_head(vb, h), _head(kh, h), _TN))
    return outs, new_states


def _hgrn_kernel(*refs, chunk, sub, t_valid, has_s0, precise):
    st_scr = refs[-HEADS:]
    a_scr = refs[-HEADS - 1]
    refs = refs[:-HEADS - 1]
    if has_s0:
        q_ref, f_ref, i_ref, g_ref, lb_ref, nw_ref, s0_ref, o_ref, s_ref = refs
    else:
        q_ref, f_ref, i_ref, g_ref, lb_ref, nw_ref, o_ref, s_ref = refs
        s0_ref = None
    c = pl.program_id(1)

    @pl.when(c == 0)
    def _():
        for h in range(HEADS):
            st_scr[h][...] = s0_ref[0, h].T if has_s0 else jnp.zeros((HEAD_W, HEAD_W), F32)

    outs, new_states = _hgrn_chunk(q_ref[...], f_ref[...], i_ref[...], lb_ref[...], [r[...] for r in st_scr],
                                   a_scr, chunk=chunk, sub=sub, t_valid=t_valid, precise=precise)
    normed = []
    for h in range(HEADS):
        st_scr[h][...] = new_states[h]
        o = outs[h]
        normed.append(o * lax.rsqrt(jnp.mean(o * o, axis=-1, keepdims=True) + EPS))
    hg = g_ref[...]
    o_all = jnp.concatenate(normed, axis=1) * nw_ref[...] * (hg * jax.nn.sigmoid(hg))
    o_ref[...] = o_all.astype(o_ref.dtype)

    @pl.when(c == pl.num_programs(1) - 1)
    def _():
        for h in range(HEADS):
            s_ref[0, h] = st_scr[h][...].T


def _hgrn(main, lbv, norm_w, s0, *, batch, chunk, sub, t_valid, precise):
    n = main.shape[0]
    n_chunks = n // batch // chunk
    slab = lambda s: pl.BlockSpec((chunk, GROUP_W), lambda b, c, s=s: (b * n_chunks + c, s))
    vec = pl.BlockSpec((1, GROUP_W), lambda b, c: (0, 0))
    state = pl.BlockSpec((1, HEADS, HEAD_W, HEAD_W), lambda b, c: (b, 0, 0, 0))
    in_specs = [slab(0), slab(1), slab(2), slab(3), vec, vec]
    args = [main, main, main, main, lbv, norm_w]
    if s0 is not None:
        in_specs.append(state)
        args.append(s0)
    return pl.pallas_call(
        functools.partial(_hgrn_kernel, chunk=chunk, sub=sub, t_valid=t_valid, has_s0=s0 is not None,
                          precise=precise),
        grid=(batch, n_chunks),
        in_specs=in_specs,
        out_specs=[pl.BlockSpec((chunk, GROUP_W), lambda b, c: (b * n_chunks + c, 0)), state],
        out_shape=[jax.ShapeDtypeStruct((n, GROUP_W), F32 if precise else BF16),
                   jax.ShapeDtypeStruct((batch, HEADS, HEAD_W, HEAD_W), F32)],
        scratch_shapes=([pltpu.VMEM((HEADS, chunk, chunk), F32)]
                        + [pltpu.VMEM((HEAD_W, HEAD_W), F32) for _ in range(HEADS)]),
        compiler_params=_params("arbitrary", "arbitrary"),
        name="hgrn",
    )(*args)


def _top_blocks(gate, n_sel):
    nb = gate.shape[0]
    blk = lax.broadcasted_iota(jnp.int32, gate.shape, 0)
    sel = jnp.zeros(gate.shape, F32)
    for _ in range(n_sel):
        mx = jnp.max(gate, axis=0, keepdims=True)
        first = jnp.min(jnp.where(gate == mx, blk, nb), axis=0, keepdims=True)
        hit = blk == first
        sel = jnp.where(hit & (mx > -jnp.inf), 1.0, sel)
        gate = jnp.where(hit, -jnp.inf, gate)
    return sel


def _moba_prompt_kernel(q_ref, k_ref, v_ref, o_ref, kb_scr, vt_scr, kmean_scr, sel_scr,
                        sa_scr, sb_scr, acc_scr, pa_scr, pb_scr, *, nb, n_sel):
    i = pl.program_id(2)
    blk = MOBA_BLOCK
    scale = HEAD_W ** -0.5

    @pl.when(i == 0)
    def _():
        def prep(n, carry):
            rows = pl.ds(pl.multiple_of(n * blk, blk), blk)
            kblk = k_ref[rows, :]
            kb_scr[n] = kblk.astype(BF16)
            kmean_scr[pl.ds(n, 1), :] = jnp.mean(kblk, axis=0, keepdims=True)
            vt_scr[n] = v_ref[rows, :].T.astype(BF16)
            return carry
        lax.fori_loop(0, nb, prep, 0)

    q = q_ref[...]
    qb = q.astype(BF16)
    if n_sel > 0:
        gate = lax.dot_general(kmean_scr[...], q, _NT, precision=HIGHEST, preferred_element_type=F32)
        bid = lax.broadcasted_iota(jnp.int32, gate.shape, 0)
        gate = jnp.where(bid < i, gate, -jnp.inf)
        sel_scr[...] = _top_blocks(gate, n_sel)

    c = scale * 1.4426950408889634

    def scores(n):
        return lax.dot_general(kb_scr[n], qb, _NT, preferred_element_type=F32)

    s = scores(i)
    kpos = lax.broadcasted_iota(jnp.int32, s.shape, 0)
    qpos = lax.broadcasted_iota(jnp.int32, s.shape, 1)
    s = jnp.where(kpos <= qpos, s, NEG_BIG)
    m = jnp.max(s, axis=0, keepdims=True)
    p = jnp.exp2((s - m) * c)
    l = jnp.sum(p, axis=0, keepdims=True)

    acc_scr[...] = jnp.zeros_like(acc_scr)
    pa_scr[...] = jnp.zeros_like(pa_scr)
    pb_scr[...] = p.astype(BF16)

    def pv(n, p_ref):
        return jnp.dot(vt_scr[n], p_ref[...], preferred_element_type=F32)

    def softmax_step(s_ref, p_ref, n, valid, m, l):
        picked = (sel_scr[pl.ds(n, 1), :] > 0.5) & valid
        s = s_ref[...]
        m_new = jnp.where(picked, jnp.maximum(m, jnp.max(s, axis=0, keepdims=True)), m)
        p = jnp.exp2((s - jnp.where(picked, m_new, -NEG_BIG)) * c)
        l = jnp.exp2((m - m_new) * c) * l + jnp.sum(p, axis=0, keepdims=True)
        p_ref[...] = p.astype(BF16)
        return m_new, l

    def body(j, carry):
        m_in, l, ma_prev, na_prev, nb_prev = carry
        n_a = 2 * j
        n_b = jnp.minimum(n_a + 1, nb - 1)
        sb_scr[...] = scores(n_b)
        pva = pv(na_prev, pa_scr)
        pvb = pv(nb_prev, pb_scr)
        m_a, l = softmax_step(sa_scr, pa_scr, n_a, True, m_in, l)
        sa_scr[...] = scores(jnp.minimum(n_a + 2, nb - 1))
        m_b, l = softmax_step(sb_scr, pb_scr, n_b, n_a + 1 < i, m_a, l)
        acc_scr[...] = (jnp.exp2((m_in - m_b) * c) * (acc_scr[...] + pvb)
                        + jnp.exp2((ma_prev - m_b) * c) * pva)
        return m_b, l, m_a, n_a, n_b

    carry = (m, l, m, jnp.int32(0), i)
    if n_sel > 0:
        @pl.when(i > 0)
        def _():
            sa_scr[...] = scores(0)
        carry = lax.fori_loop(0, (i + 1) // 2, body, carry)
    m_f, l, ma_last, na_last, nb_last = carry
    acc = acc_scr[...] + pv(nb_last, pb_scr) + jnp.exp2((ma_last - m_f) * c) * pv(na_last, pa_scr)
    o_ref[...] = (acc / l).T.astype(BF16)


def _moba_prompt(main, k, v, *, batch):
    n = main.shape[0]
    seq = n // batch
    nb = seq // MOBA_BLOCK
    n_sel = min(MOBA_TOPK, nb - 1)
    q_col0 = 4 * HEADS
    return pl.pallas_call(
        functools.partial(_moba_prompt_kernel, nb=nb, n_sel=n_sel),
        grid=(batch, HEADS, nb),
        in_specs=[
            pl.BlockSpec((MOBA_BLOCK, HEAD_W), lambda b, h, i: (b * nb + i, q_col0 + h)),
            pl.BlockSpec((seq, HEAD_W), lambda b, h, i: (b, h)),
            pl.BlockSpec((seq, HEAD_W), lambda b, h, i: (b, h)),
        ],
        out_specs=pl.BlockSpec((MOBA_BLOCK, HEAD_W), lambda b, h, i: (b * nb + i, h)),
        out_shape=jax.ShapeDtypeStruct((n, GROUP_W), BF16),
        scratch_shapes=[
            pltpu.VMEM((nb, MOBA_BLOCK, HEAD_W), BF16),
            pltpu.VMEM((nb, HEAD_W, MOBA_BLOCK), BF16),
            pltpu.VMEM((max(nb, 8), HEAD_W), F32),
            pltpu.VMEM((max(nb, 8), MOBA_BLOCK), F32),
            pltpu.VMEM((MOBA_BLOCK, MOBA_BLOCK), F32),
            pltpu.VMEM((MOBA_BLOCK, MOBA_BLOCK), F32),
            pltpu.VMEM((HEAD_W, MOBA_BLOCK), F32),
            pltpu.VMEM((MOBA_BLOCK, MOBA_BLOCK), BF16),
            pltpu.VMEM((MOBA_BLOCK, MOBA_BLOCK), BF16),
        ],
        compiler_params=_params("arbitrary", "arbitrary", "arbitrary"),
        name="moba_prompt",
    )(main, k, v)


def _route(logits):
    lane = lax.broadcasted_iota(jnp.int32, logits.shape, 1)
    is_g = lane < N_GROUPS
    gl = jnp.where(is_g, logits, -jnp.inf)
    gmax = jnp.max(gl, axis=-1, keepdims=True)
    g_sel = jnp.min(jnp.where(gl == gmax, lane, LANES), axis=-1, keepdims=True)
    p_group = 1.0 / jnp.sum(jnp.where(is_g, jnp.exp(gl - gmax), 0.0), axis=-1, keepdims=True)
    e_lo = N_GROUPS + EXPERTS_PER_GROUP * g_sel
    is_e = (lane >= e_lo) & (lane < e_lo + EXPERTS_PER_GROUP)
    el = jnp.where(is_e, logits, -jnp.inf)
    emax = jnp.max(el, axis=-1, keepdims=True)
    ee = jnp.where(is_e, jnp.exp(el - emax), 0.0)
    prob = ee / jnp.sum(ee, axis=-1, keepdims=True)
    prob = jnp.where(is_e, prob, -1.0)
    p1 = jnp.max(prob, axis=-1, keepdims=True)
    i1 = jnp.min(jnp.where(prob == p1, lane, LANES), axis=-1, keepdims=True)
    prob2 = jnp.where(lane == i1, -1.0, prob)
    p2 = jnp.max(prob2, axis=-1, keepdims=True)
    i2 = jnp.min(jnp.where(prob2 == p2, lane, LANES), axis=-1, keepdims=True)
    tot = p1 + p2
    w1 = p_group * (p1 / tot)
    w2 = p_group * (p2 / tot)
    e1 = (i1 - N_GROUPS).astype(F32)
    e2 = (i2 - N_GROUPS).astype(F32)
    return jnp.where(lane == 0, e1, jnp.where(lane == 1, e2, jnp.where(lane == 2, w1, jnp.where(lane == 3, w2, 0.0))))


ROW_TILES = D_MODEL // LANES


def _store_rows(ref, x):
    for j in range(ROW_TILES):
        ref[:, j, :] = x[:, j * LANES:(j + 1) * LANES]


def _load_planes(ref, rows=slice(None)):
    return jnp.concatenate([ref[j, rows, :] for j in range(ROW_TILES)], axis=1)


def _outproj_kernel(x_ref, oh_ref, oa_ref, w_ref, nw_ref, wr_ref, x1_ref, hn_ref, route_ref, *, precise):
    y = _mm(oh_ref[...], w_ref[0:GROUP_W, :], precise=precise)
    y = y + _mm(oa_ref[...], w_ref[GROUP_W:2 * GROUP_W, :], precise=precise)
    x1 = x_ref[...] + y
    x1_ref[...] = x1
    ms = jnp.mean(x1 * x1, axis=-1, keepdims=True)
    hn = x1 * lax.rsqrt(ms + EPS) * nw_ref[...]
    _store_rows(hn_ref, hn)
    if precise:
        logits = jnp.dot(hn, wr_ref[...], precision=HIGHEST, preferred_element_type=F32)
    else:
        logits = _mm3(hn, wr_ref[...])
    route_ref[...] = _route(logits)


def _outproj(x, oh, oa, w_out, norm_w, w_router):
    n = x.shape[0]
    tm = min(n, 512)
    row = lambda w: pl.BlockSpec((tm, w), lambda i: (i, 0))
    full = lambda a, b: pl.BlockSpec((a, b), lambda i: (0, 0), pipeline_mode=pl.Buffered(1))
    return pl.pallas_call(
        functools.partial(_outproj_kernel, precise=w_out.dtype == F32),
        grid=(n // tm,),
        in_specs=[row(D_MODEL), row(GROUP_W), row(GROUP_W), full(2 * GROUP_W, D_MODEL), full(1, D_MODEL),
                  full(D_MODEL, LANES)],
        out_specs=[row(D_MODEL), pl.BlockSpec((tm, ROW_TILES, LANES), lambda i: (i, 0, 0)), row(LANES)],
        out_shape=[jax.ShapeDtypeStruct((n, D_MODEL), F32), jax.ShapeDtypeStruct((n, ROW_TILES, LANES), F32),
                   jax.ShapeDtypeStruct((n, LANES), F32)],
        compiler_params=_params("arbitrary"),
        name="outproj",
    )(x, oh, oa, w_out, norm_w, w_router)


def _moe_kernel(te_ref, tv_ref, rt_ref, hn_hbm, wg_ref, wu_ref, wd_ref, y_ref, xbuf, sem):
    t = pl.program_id(0)
    nt = pl.num_programs(0)
    slot = t % 2
    rows = xbuf.shape[2]

    def row_copy(tile, r, slot):
        tok = rt_ref[tile * rows + r]
        return pltpu.make_async_copy(hn_hbm.at[tok], xbuf.at[slot, :, r, :], sem.at[slot])

    def issue(tile, slot):
        def body(r, carry):
            row_copy(tile, r, slot).start()
            return carry
        lax.fori_loop(0, rows, body, 0, unroll=8)

    @pl.when(t == 0)
    def _():
        issue(0, 0)

    @pl.when(t + 1 < nt)
    def _():
        issue(t + 1, 1 - slot)

    def wait_body(r, carry):
        row_copy(t, r, slot).wait()
        return carry
    lax.fori_loop(0, rows, wait_body, 0, unroll=8)

    @pl.when(tv_ref[t] > 0)
    def _():
        x = _load_planes(xbuf.at[slot]).astype(BF16)
        a = jnp.dot(x, wg_ref[0], preferred_element_type=F32)
        b = jnp.dot(x, wu_ref[0], preferred_element_type=F32)
        act = (a * jax.nn.sigmoid(a)) * b
        _store_rows(y_ref, jnp.dot(act.astype(BF16), wd_ref[0], preferred_element_type=F32))

    @pl.when(tv_ref[t] == 0)
    def _():
        y_ref[...] = jnp.zeros_like(y_ref)


def _moe(hn, tile_expert, tile_valid, row_token, wg, wu, wd):
    n_rows = row_token.shape[0]
    n_tiles = tile_expert.shape[0]
    rows = n_rows // n_tiles
    grid_spec = pltpu.PrefetchScalarGridSpec(
        num_scalar_prefetch=3,
        grid=(n_tiles,),
        in_specs=[
            pl.BlockSpec(memory_space=pl.ANY),
            pl.BlockSpec((1, D_MODEL, EXPERT_FF), lambda t, te, tv, rt: (te[t], 0, 0)),
            pl.BlockSpec((1, D_MODEL, EXPERT_FF), lambda t, te, tv, rt: (te[t], 0, 0)),
            pl.BlockSpec((1, EXPERT_FF, D_MODEL), lambda t, te, tv, rt: (te[t], 0, 0)),
        ],
        out_specs=pl.BlockSpec((rows, ROW_TILES, LANES), lambda t, te, tv, rt: (t, 0, 0)),
        scratch_shapes=[pltpu.VMEM((2, ROW_TILES, rows, LANES), F32), pltpu.SemaphoreType.DMA((2,))],
    )
    return pl.pallas_call(
        _moe_kernel,
        grid_spec=grid_spec,
        out_shape=jax.ShapeDtypeStruct((n_rows, ROW_TILES, LANES), F32),
        compiler_params=_params("arbitrary"),
        name="moe",
    )(tile_expert, tile_valid, row_token, hn, wg, wu, wd)


def _moe_plan(route, tile):
    n = route.shape[0]
    n_rows = 2 * n + N_EXPERTS * tile
    ids = route[:, 0:2].astype(jnp.int32).reshape(-1)
    onehot = (ids[:, None] == jnp.arange(N_EXPERTS, dtype=jnp.int32)[None, :]).astype(jnp.int32)
    csum = jnp.cumsum(onehot, axis=0)
    rank = jnp.take_along_axis(csum, ids[:, None], axis=1)[:, 0] - 1
    counts = csum[-1]
    padded = ((counts + tile - 1) // tile) * tile
    ends = jnp.cumsum(padded)
    starts = ends - padded
    dest = starts[ids] + rank
    token = jnp.arange(2 * n, dtype=jnp.int32) // 2
    row_token = jnp.zeros((n_rows,), jnp.int32).at[dest].set(token)
    tile_start = jnp.arange(n_rows // tile, dtype=jnp.int32) * tile
    tile_expert = jnp.sum((tile_start[:, None] >= ends[None, :]).astype(jnp.int32), axis=1)
    tile_valid = (tile_expert < N_EXPERTS).astype(jnp.int32)
    last = jnp.max(jnp.where(counts > 0, jnp.arange(N_EXPERTS, dtype=jnp.int32), 0))
    tile_expert = jnp.where(tile_valid > 0, tile_expert, last).astype(jnp.int32)
    return tile_expert, tile_valid, row_token, dest


def _combine_kernel(dest_ref, x1_ref, route_ref, y_hbm, nw_ref, o_ref, ybuf, sem, *, tm):
    t = pl.program_id(0)
    nt = pl.num_programs(0)
    slot = t % 2

    def row_copy(tile, r, slot):
        src = dest_ref[tile * (2 * tm) + r]
        return pltpu.make_async_copy(y_hbm.at[src], ybuf.at[slot, :, r, :], sem.at[slot])

    def issue(tile, slot):
        def body(r, carry):
            row_copy(tile, r, slot).start()
            return carry
        lax.fori_loop(0, 2 * tm, body, 0, unroll=8)

    @pl.when(t == 0)
    def _():
        issue(0, 0)

    @pl.when(t + 1 < nt)
    def _():
        issue(t + 1, 1 - slot)

    def wait_body(r, carry):
        row_copy(t, r, slot).wait()
        return carry
    lax.fori_loop(0, 2 * tm, wait_body, 0, unroll=8)

    route = route_ref[...]
    y0 = _load_planes(ybuf.at[slot], slice(0, tm))
    y1 = _load_planes(ybuf.at[slot], slice(tm, 2 * tm))
    x = x1_ref[...] + (route[:, 2:3] * y0 + route[:, 3:4] * y1)
    ms = jnp.mean(x * x, axis=-1, keepdims=True)
    o_ref[...] = x * lax.rsqrt(ms + EPS) * nw_ref[...]


COMBINE_TILE = 128


def _combine(x1, route, y_sorted, dest_tiled, norm_w):
    n = x1.shape[0]
    tm = min(n, COMBINE_TILE)
    grid_spec = pltpu.PrefetchScalarGridSpec(
        num_scalar_prefetch=1,
        grid=(n // tm,),
        in_specs=[
            pl.BlockSpec((tm, D_MODEL), lambda t, d: (t, 0)),
            pl.BlockSpec((tm, LANES), lambda t, d: (t, 0)),
            pl.BlockSpec(memory_space=pl.ANY),
            pl.BlockSpec((1, D_MODEL), lambda t, d: (0, 0)),
        ],
        out_specs=pl.BlockSpec((tm, D_MODEL), lambda t, d: (t, 0)),
        scratch_shapes=[pltpu.VMEM((2, ROW_TILES, 2 * tm, LANES), F32), pltpu.SemaphoreType.DMA((2,))],
    )
    return pl.pallas_call(
        functools.partial(_combine_kernel, tm=tm),
        grid_spec=grid_spec,
        out_shape=jax.ShapeDtypeStruct((n, D_MODEL), F32),
        compiler_params=_params("arbitrary"),
        name="combine",
    )(dest_tiled, x1, route, y_sorted, norm_w)


def _ffn_and_final(x1, hn, route, wg, wu, wd, norm_final):
    n = x1.shape[0]
    tile = min(MOE_TILE, max(MOE_TILE_MIN, n // 4))
    tile_expert, tile_valid, row_token, dest = _moe_plan(route, tile)
    y_sorted = _moe(hn, tile_expert, tile_valid, row_token, wg, wu, wd)
    tm = min(n, COMBINE_TILE)
    dest_tiled = dest.reshape(n // tm, tm, 2).transpose(0, 2, 1).reshape(-1)
    return _combine(x1, route, y_sorted, dest_tiled, norm_final)


def _kmean_kernel(pt_ref, *refs, pages_per_step):
    page_refs = refs[:pages_per_step]
    o_ref = refs[pages_per_step]
    j = pl.program_id(1)
    ppb = MOBA_BLOCK // PAGE_SIZE
    for blk in range(pages_per_step // ppb):
        tot = jnp.sum(page_refs[blk * ppb][0], axis=0)
        for pg in range(1, ppb):
            tot = tot + jnp.sum(page_refs[blk * ppb + pg][0], axis=0)
        o_ref[0, j * (pages_per_step // ppb) + blk] = tot * (1.0 / MOBA_BLOCK)


def _kmean(cache_k, page_table):
    dec_batch, n_pages = page_table.shape
    ppb = MOBA_BLOCK // PAGE_SIZE
    n_blocks = n_pages // ppb
    pages_per_step = 8
    steps = n_pages // pages_per_step

    def page_spec(p):
        return pl.BlockSpec((1, PAGE_SIZE, HEADS, HEAD_W),
                            lambda s, j, pt, p=p: (pt[s * n_pages + j * pages_per_step + p], 0, 0, 0))

    grid_spec = pltpu.PrefetchScalarGridSpec(
        num_scalar_prefetch=1,
        grid=(dec_batch, steps),
        in_specs=[page_spec(p) for p in range(pages_per_step)],
        out_specs=pl.BlockSpec((1, n_blocks, HEADS, HEAD_W), lambda s, j, pt: (s, 0, 0, 0)),
    )
    return pl.pallas_call(
        functools.partial(_kmean_kernel, pages_per_step=pages_per_step),
        grid_spec=grid_spec,
        out_shape=jax.ShapeDtypeStruct((dec_batch, n_blocks, HEADS, HEAD_W), F32),
        compiler_params=_params("arbitrary", "arbitrary"),
        name="kmean",
    )(page_table.reshape(-1), *([cache_k] * pages_per_step))


def _sample_topk_kernel(q_ref, kmean_ref, idx_ref, q_scr, *, t_new, n_sel):
    q_scr[...] = jnp.zeros_like(q_scr)
    for h in range(HEADS):
        sl = slice(h * HEAD_W, (h + 1) * HEAD_W)
        q_scr[0:t_new, :] = q_ref[0, :, sl]
        gate = lax.dot_general(kmean_ref[0, :, h, :], q_scr[...], _NT, precision=HIGHEST,
                               preferred_element_type=F32)
        nb = gate.shape[0]
        blk = lax.broadcasted_iota(jnp.int32, gate.shape, 0)
        row = lax.broadcasted_iota(jnp.int32, (8, LANES), 0)
        tile = jnp.zeros((8, LANES), jnp.int32)
        for r in range(n_sel):
            mx = jnp.max(gate, axis=0, keepdims=True)
            first = jnp.min(jnp.where(gate == mx, blk, nb), axis=0, keepdims=True)
            tile = jnp.where(row == r, jnp.broadcast_to(first, (8, LANES)), tile)
            gate = jnp.where(blk == first, -jnp.inf, gate)
        idx_ref[0, h] = tile


def _sample_topk(main3, kmean, *, n_sel):
    dec_batch, t_new, _ = main3.shape
    n_blocks = kmean.shape[1]
    return pl.pallas_call(
        functools.partial(_sample_topk_kernel, t_new=t_new, n_sel=n_sel),
        grid=(dec_batch,),
        in_specs=[
            pl.BlockSpec((1, t_new, GROUP_W), lambda s: (s, 0, 4)),
            pl.BlockSpec((1, n_blocks, HEADS, HEAD_W), lambda s: (s, 0, 0, 0)),
        ],
        out_specs=pl.BlockSpec((1, HEADS, 8, LANES), lambda s: (s, 0, 0, 0)),
        out_shape=jax.ShapeDtypeStruct((dec_batch, HEADS, 8, LANES), jnp.int32),
        scratch_shapes=[pltpu.VMEM((LANES, HEAD_W), F32)],
        compiler_params=_params("arbitrary"),
        name="sample_topk",
    )(main3, kmean)


def _sample_attn_kernel(idx_ref, pt_ref, q_ref, kn_ref, vn_ref, ck_hbm, cv_hbm, o_ref,
                        kbuf, vbuf, q_scr, kn_scr, vn_scr, sem, *, t_new, n_sel, n_pages):
    s = pl.program_id(0)
    h = pl.program_id(1)
    step = s * HEADS + h
    n_steps = pl.num_programs(0) * HEADS
    slot = step % 2
    ppb = MOBA_BLOCK // PAGE_SIZE
    scale = HEAD_W ** -0.5

    def copies(step, slot):
        s_ = step // HEADS
        h_ = step % HEADS
        out = []
        for t in range(t_new):
            for r in range(n_sel):
                b = idx_ref[(step * t_new + t) * n_sel + r]
                for pg in range(ppb):
                    page = pt_ref[s_ * n_pages + b * ppb + pg]
                    rows = pl.ds((r * ppb + pg) * PAGE_SIZE, PAGE_SIZE)
                    out.append(pltpu.make_async_copy(ck_hbm.at[page, :, h_, :], kbuf.at[slot, t, rows, :],
                                                     sem.at[slot]))
                    out.append(pltpu.make_async_copy(cv_hbm.at[page, :, h_, :], vbuf.at[slot, t, rows, :],
                                                     sem.at[slot]))
        return out

    @pl.when(step == 0)
    def _():
        for c in copies(0, 0):
            c.start()

    @pl.when(step + 1 < n_steps)
    def _():
        for c in copies(step + 1, 1 - slot):
            c.start()

    q_scr[...] = jnp.zeros_like(q_scr)
    kn_scr[...] = jnp.zeros_like(kn_scr)
    vn_scr[...] = jnp.zeros_like(vn_scr)
    q_scr[0:t_new, :] = q_ref[0]
    kn_scr[0:t_new, :] = kn_ref[0]
    vn_scr[0:t_new, :] = vn_ref[0]
    mm = _mm3
    qb = q_scr[...]
    s_new = mm(qb, kn_scr[...], _NT) * scale
    trow = lax.broadcasted_iota(jnp.int32, s_new.shape, 0)
    jcol = lax.broadcasted_iota(jnp.int32, s_new.shape, 1)
    new_ok = (jcol <= trow) & (jcol < t_new)
    s_new = jnp.where(new_ok, s_new, NEG_BIG)
    m_new = jnp.max(s_new, axis=-1, keepdims=True)

    for c in copies(step, slot):
        c.wait()

    n_keys = n_sel * MOBA_BLOCK
    st = mm(qb, kbuf[slot].reshape(t_new * n_keys, HEAD_W), _NT) * scale
    srow = lax.broadcasted_iota(jnp.int32, st.shape, 0)
    scol = lax.broadcasted_iota(jnp.int32, st.shape, 1)
    own = (scol >= srow * n_keys) & (scol < (srow + 1) * n_keys)
    st = jnp.where(own, st, NEG_BIG)
    m = jnp.maximum(jnp.max(st, axis=-1, keepdims=True), m_new)
    p = jnp.exp(st - m)
    pn = jnp.where(new_ok, jnp.exp(s_new - m), 0.0)
    den = jnp.sum(p, axis=-1, keepdims=True) + jnp.sum(pn, axis=-1, keepdims=True)
    out = (mm(p, vbuf[slot].reshape(t_new * n_keys, HEAD_W)) + mm(pn, vn_scr[...])) / den
    o_ref[0] = out[0:t_new, :]


def _sample_attn(idx, page_table, main3, kn3, vn3, cache_k4, cache_v4, *, n_sel):
    dec_batch, t_new, _ = main3.shape
    n_pages = page_table.shape[1]
    q_col0 = 4 * HEADS
    grid_spec = pltpu.PrefetchScalarGridSpec(
        num_scalar_prefetch=2,
        grid=(dec_batch, HEADS),
        in_specs=[
            pl.BlockSpec((1, t_new, HEAD_W), lambda s, h, ix, pt: (s, 0, q_col0 + h)),
            pl.BlockSpec((1, t_new, HEAD_W), lambda s, h, ix, pt: (s, 0, h)),
            pl.BlockSpec((1, t_new, HEAD_W), lambda s, h, ix, pt: (s, 0, h)),
            pl.BlockSpec(memory_space=pl.ANY),
            pl.BlockSpec(memory_space=pl.ANY),
        ],
        out_specs=pl.BlockSpec((1, t_new, HEAD_W), lambda s, h, ix, pt: (s, 0, h)),
        scratch_shapes=[
            pltpu.VMEM((2, t_new, n_sel * MOBA_BLOCK, HEAD_W), F32),
            pltpu.VMEM((2, t_new, n_sel * MOBA_BLOCK, HEAD_W), F32),
            pltpu.VMEM((8, HEAD_W), F32),
            pltpu.VMEM((16, HEAD_W), F32),
            pltpu.VMEM((16, HEAD_W), F32),
            pltpu.SemaphoreType.DMA((2,)),
        ],
    )
    return pl.pallas_call(
        functools.partial(_sample_attn_kernel, t_new=t_new, n_sel=n_sel, n_pages=n_pages),
        grid_spec=grid_spec,
        out_shape=jax.ShapeDtypeStruct((dec_batch, t_new, GROUP_W), F32),
        compiler_params=_params("arbitrary", "arbitrary"),
        name="sample_attn",
    )(idx, page_table.reshape(-1), main3, kn3, vn3, cache_k4, cache_v4)


def _moba_sample(main, kn, vn, cache_k, cache_v, page_table):
    dec_batch, n_pages = page_table.shape
    t_new = main.shape[0] // dec_batch
    n_blocks = n_pages * PAGE_SIZE // MOBA_BLOCK
    n_sel = min(MOBA_TOPK, n_blocks)
    main3 = main.reshape(dec_batch, t_new, MAIN_W)
    kn3 = kn.reshape(dec_batch, t_new, GROUP_W)
    vn3 = vn.reshape(dec_batch, t_new, GROUP_W)
    kmean = _kmean(cache_k, page_table)
    idx = _sample_topk(main3, kmean, n_sel=n_sel)
    idx = idx[:, :, :n_sel, :t_new].transpose(0, 1, 3, 2).reshape(-1)
    o = _sample_attn(idx, page_table, main3, kn3, vn3, cache_k, cache_v, n_sel=n_sel)
    return o.reshape(dec_batch * t_new, GROUP_W)


def _layer(x, attend, hgrn_fn, weights):
    norm_mix, w_in, lbv, hgrn_norm, w_out, norm_ffn, w_router, wg, wu, wd, norm_final = weights
    main, k, v = _inproj(x, norm_mix, w_in)
    oh, state = hgrn_fn(main, lbv, hgrn_norm)
    oa = attend(main, k, v)
    x1, hn, route = _outproj(x, oh, oa, w_out, norm_ffn, w_router)
    y = _ffn_and_final(x1, hn, route, wg, wu, wd, norm_final)
    return y, k, v, state


def kernel(x_prompt, x_sample, cache_k, cache_v, state_hgrn, page_table, norm_mix, w_in, lb_logits, hgrn_norm,
           w_out, norm_ffn, w_group, w_expert, w_gate, w_up, w_down, norm_final):
    depth = w_in.shape[0]
    assert depth == 1, "the final norm is fused into the (single) layer"
    batch, seq, _ = x_prompt.shape
    dec_batch, t_new, _ = x_sample.shape
    lb_all = jnp.cumsum(jax.nn.softmax(lb_logits.astype(F32), axis=0), axis=0)
    layer = 0
    w_router = jnp.concatenate(
        [w_group[layer], w_expert[layer], jnp.zeros((D_MODEL, LANES - N_GROUPS - N_EXPERTS), F32)], axis=1)
    def weights(mix_dtype):
        return (
            norm_mix[layer][None, :], w_in[layer].astype(mix_dtype), lb_all[layer][None, :],
            hgrn_norm[layer][None, :], w_out[layer].astype(mix_dtype), norm_ffn[layer][None, :], w_router,
            w_gate[layer].reshape(N_EXPERTS, D_MODEL, EXPERT_FF).astype(BF16),
            w_up[layer].reshape(N_EXPERTS, D_MODEL, EXPERT_FF).astype(BF16),
            w_down[layer].reshape(N_EXPERTS, EXPERT_FF, D_MODEL).astype(BF16),
            norm_final[None, :],
        )

    hgrn_p = functools.partial(_hgrn, s0=None, batch=batch, chunk=HGRN_CHUNK, sub=HGRN_SUB, t_valid=HGRN_CHUNK,
                               precise=False)
    moba_p = functools.partial(_moba_prompt, batch=batch)
    yp, kp, vp, sp = _layer(x_prompt.reshape(batch * seq, D_MODEL), moba_p, hgrn_p, weights(BF16))

    pad = 8

    def hgrn_s(main, lbv, nw):
        m3 = main.reshape(dec_batch, t_new, MAIN_W)
        m3 = jnp.pad(m3, ((0, 0), (0, pad - t_new), (0, 0))).reshape(dec_batch * pad, MAIN_W)
        o, st = _hgrn(m3, lbv, nw, state_hgrn[layer], batch=dec_batch, chunk=pad, sub=pad, t_valid=t_new,
                      precise=True)
        return o.reshape(dec_batch, pad, GROUP_W)[:, :t_new].reshape(dec_batch * t_new, GROUP_W), st

    moba_s = functools.partial(_moba_sample, cache_k=cache_k.reshape(cache_k.shape[1:]),
                               cache_v=cache_v.reshape(cache_v.shape[1:]), page_table=page_table)
    ys, ks, vs, ss = _layer(x_sample.reshape(dec_batch * t_new, D_MODEL), moba_s, hgrn_s, weights(F32))

    return (yp.reshape(batch, seq, D_MODEL), ys.reshape(dec_batch, t_new, D_MODEL),
            kp.reshape(1, batch, seq, HEADS, HEAD_W), vp.reshape(1, batch, seq, HEADS, HEAD_W), sp[None],
            ks.reshape(1, dec_batch, t_new, HEADS, HEAD_W), vs.reshape(1, dec_batch, t_new, HEADS, HEAD_W), ss[None])
```

```python
import functools

import jax
import jax.numpy as jnp
from jax import lax
from jax.experimental import pallas as pl
from jax.experimental.pallas import tpu as pltpu

F32 = jnp.float32
BF16 = jnp.bfloat16
HIGHEST = lax.Precision.HIGHEST

D_MODEL = 2048
HEADS = 8
HEAD_W = 128
GROUP_W = HEADS * HEAD_W
MAIN_W = 5 * GROUP_W
HGRN_CHUNK = 64
HGRN_SUB = 16
MOBA_BLOCK = 256
MOBA_TOPK = 3
PAGE_SIZE = 128
N_GROUPS = 4
EXPERTS_PER_GROUP = 8
N_EXPERTS = N_GROUPS * EXPERTS_PER_GROUP
EXPERT_FF = 256
EPS = 1e-6
LANES = 128
MOE_TILE = 256
MOE_TILE_MIN = 32
VMEM_LIMIT = 52 * 1024 * 1024
NEG_BIG = -1e30

_NT = (((1,), (1,)), ((), ()))
_TN = (((0,), (0,)), ((), ()))


def _params(*sem):
    return pltpu.CompilerParams(dimension_semantics=sem, vmem_limit_bytes=VMEM_LIMIT)


def _mm(a, b, dims=None, *, precise):
    if dims is None:
        dims = (((a.ndim - 1,), (0,)), ((), ()))
    if precise:
        return lax.dot_general(a.astype(F32), b.astype(F32), dims, precision=HIGHEST, preferred_element_type=F32)
    return lax.dot_general(a.astype(BF16), b.astype(BF16), dims, preferred_element_type=F32)


def _mm3(a, b, dims=None):
    if dims is None:
        dims = (((a.ndim - 1,), (0,)), ((), ()))

    def halves(x):
        hi = x.astype(BF16)
        return hi, (x - hi.astype(F32)).astype(BF16)

    (ah, al), (bh, bl) = halves(a), halves(b)
    dot = lambda x, y: lax.dot_general(x, y, dims, preferred_element_type=F32)
    return dot(ah, bh) + (dot(ah, bl) + dot(al, bh))


def _inproj_kernel(x_ref, nw_ref, w_ref, main_ref, k_ref, v_ref, h_scr, *, n_main, n_kv, precise):
    j = pl.program_id(1)

    @pl.when(j == 0)
    def _():
        x = x_ref[...]
        ms = jnp.mean(x * x, axis=-1, keepdims=True)
        h_scr[...] = (x * lax.rsqrt(ms + EPS) * nw_ref[...]).astype(h_scr.dtype)

    acc = _mm(h_scr[...], w_ref[...], precise=precise)

    @pl.when(j < n_main)
    def _():
        main_ref[...] = acc

    @pl.when((j >= n_main) & (j < n_main + n_kv))
    def _():
        k_ref[...] = acc

    @pl.when(j >= n_main + n_kv)
    def _():
        v_ref[...] = acc


def _inproj(x, norm_w, w_in):
    n = x.shape[0]
    precise = w_in.dtype == F32
    tm = min(n, 1024)
    tn = 512
    n_main = MAIN_W // tn
    n_kv = GROUP_W // tn
    grid = (n // tm, n_main + 2 * n_kv)
    return pl.pallas_call(
        functools.partial(_inproj_kernel, n_main=n_main, n_kv=n_kv, precise=precise),
        grid=grid,
        in_specs=[
            pl.BlockSpec((tm, D_MODEL), lambda i, j: (i, 0)),
            pl.BlockSpec((1, D_MODEL), lambda i, j: (0, 0)),
            pl.BlockSpec((D_MODEL, tn), lambda i, j: (0, j)),
        ],
        out_specs=[
            pl.BlockSpec((tm, tn), lambda i, j: (i, jnp.minimum(j, n_main - 1))),
            pl.BlockSpec((tm, tn), lambda i, j: (i, jnp.clip(j - n_main, 0, n_kv - 1))),
            pl.BlockSpec((tm, tn), lambda i, j: (i, jnp.clip(j - n_main - n_kv, 0, n_kv - 1))),
        ],
        out_shape=[
            jax.ShapeDtypeStruct((n, MAIN_W), F32),
            jax.ShapeDtypeStruct((n, GROUP_W), F32),
            jax.ShapeDtypeStruct((n, GROUP_W), F32),
        ],
        scratch_shapes=[pltpu.VMEM((tm, D_MODEL), w_in.dtype)],
        compiler_params=_params("arbitrary", "arbitrary"),
        name="inproj",
    )(x, norm_w, w_in)


def _head(x, h):
    return x[:, h * HEAD_W:(h + 1) * HEAD_W]


def _hgrn_chunk(hq, hf, hi, lbv, states, a_scr, *, chunk, sub, t_valid, precise):
    mm = functools.partial(_mm, precise=precise)
    cast = (lambda x: x) if precise else (lambda x: x.astype(BF16))
    g = jnp.log(lbv + (1.0 - lbv) * jax.nn.sigmoid(hf))
    k = (1.0 - lbv) * jax.nn.sigmoid(-hf)
    q = hq * jax.nn.sigmoid(hq)
    if t_valid < chunk:
        live = lax.broadcasted_iota(jnp.int32, g.shape, 0) < t_valid
        g = jnp.where(live, g, 0.0)
        k = jnp.where(live, k, 0.0)
    row = lax.broadcasted_iota(jnp.int32, (chunk, chunk), 0)
    col = lax.broadcasted_iota(jnp.int32, (chunk, chunk), 1)
    tri = (col <= row).astype(BF16)
    g1 = g.astype(BF16)
    r1 = g - g1.astype(F32)
    g2 = r1.astype(BF16)
    g3 = (r1 - g2.astype(F32)).astype(BF16)
    tdot = lambda x: jnp.dot(tri, x, preferred_element_type=F32)
    b = tdot(g1) + (tdot(g2) + tdot(g3))
    vb = cast(hi)
    krow = lax.broadcasted_iota(jnp.int32, (chunk, GROUP_W), 0)
    for blk in range(chunk // sub):
        r0, r1 = blk * sub, (blk + 1) * sub
        beta = b[r0 - 1:r0] if blk > 0 else jnp.zeros((1, GROUP_W), F32)
        qt = cast(q[r0:r1] * jnp.exp(b[r0:r1] - beta))
        kt = cast(k * jnp.exp(jnp.where(krow < r1, beta - b, -jnp.inf)))
        trow = lax.broadcasted_iota(jnp.int32, (sub, chunk), 0) + r0
        scol = lax.broadcasted_iota(jnp.int32, (sub, chunk), 1)
        for h in range(HEADS):
            a_scr[h, r0:r1, :] = jnp.where(scol <= trow, mm(_head(qt, h), _head(kt, h), _NT), 0.0)
    qh = cast(q * jnp.exp(b))
    b_last = b[chunk - 1:chunk]
    kh = cast(k * jnp.exp(b_last - b))
    decay = jnp.exp(b_last)
    outs, new_states = [], []
    for h in range(HEADS):
        outs.append(mm(a_scr[h], _head(vb, h)) + mm(_head(qh, h), states[h], _NT))
        new_states.append(states[h] * _head(decay, h) + mm(_head(vb, h), _head(kh, h), _TN))
    return outs, new_states


def _hgrn_kernel(*refs, chunk, sub, t_valid, has_s0, precise):
    st_scr = refs[-HEADS:]
    a_scr = refs[-HEADS - 1]
    refs = refs[:-HEADS - 1]
    if has_s0:
        q_ref, f_ref, i_ref, g_ref, lb_ref, nw_ref, s0_ref, o_ref, s_ref = refs
    else:
        q_ref, f_ref, i_ref, g_ref, lb_ref, nw_ref, o_ref, s_ref = refs
        s0_ref = None
    c = pl.program_id(1)

    @pl.when(c == 0)
    def _():
        for h in range(HEADS):
            st_scr[h][...] = s0_ref[0, h].T if has_s0 else jnp.zeros((HEAD_W, HEAD_W), F32)

    outs, new_states = _hgrn_chunk(q_ref[...], f_ref[...], i_ref[...], lb_ref[...], [r[...] for r in st_scr],
                                   a_scr, chunk=chunk, sub=sub, t_valid=t_valid, precise=precise)
    normed = []
    for h in range(HEADS):
        st_scr[h][...] = new_states[h]
        o = outs[h]
        normed.append(o * lax.rsqrt(jnp.mean(o * o, axis=-1, keepdims=True) + EPS))
    hg = g_ref[...]
    o_all = jnp.concatenate(normed, axis=1) * nw_ref[...] * (hg * jax.nn.sigmoid(hg))
    o_ref[...] = o_all.astype(o_ref.dtype)

    @pl.when(c == pl.num_programs(1) - 1)
    def _():
        for h in range(HEADS):
            s_ref[0, h] = st_scr[h][...].T


def _hgrn(main, lbv, norm_w, s0, *, batch, chunk, sub, t_valid, precise):
    n = main.shape[0]
    n_chunks = n // batch // chunk
    slab = lambda s: pl.BlockSpec((chunk, GROUP_W), lambda b, c, s=s: (b * n_chunks + c, s))
    vec = pl.BlockSpec((1, GROUP_W), lambda b, c: (0, 0))
    state = pl.BlockSpec((1, HEADS, HEAD_W, HEAD_W), lambda b, c: (b, 0, 0, 0))
    in_specs = [slab(0), slab(1), slab(2), slab(3), vec, vec]
    args = [main, main, main, main, lbv, norm_w]
    if s0 is not None:
        in_specs.append(state)
        args.append(s0)
    return pl.pallas_call(
        functools.partial(_hgrn_kernel, chunk=chunk, sub=sub, t_valid=t_valid, has_s0=s0 is not None,
                          precise=precise),
        grid=(batch, n_chunks),
        in_specs=in_specs,
        out_specs=[pl.BlockSpec((chunk, GROUP_W), lambda b, c: (b * n_chunks + c, 0)), state],
        out_shape=[jax.ShapeDtypeStruct((n, GROUP_W), F32 if precise else BF16),
                   jax.ShapeDtypeStruct((batch, HEADS, HEAD_W, HEAD_W), F32)],
        scratch_shapes=([pltpu.VMEM((HEADS, chunk, chunk), F32)]
                        + [pltpu.VMEM((HEAD_W, HEAD_W), F32) for _ in range(HEADS)]),
        compiler_params=_params("arbitrary", "arbitrary"),
        name="hgrn",
    )(*args)


def _top_blocks(gate, n_sel):
    nb = gate.shape[0]
    blk = lax.broadcasted_iota(jnp.int32, gate.shape, 0)
    sel = jnp.zeros(gate.shape, F32)
    for _ in range(n_sel):
        mx = jnp.max(gate, axis=0, keepdims=True)
        first = jnp.min(jnp.where(gate == mx, blk, nb), axis=0, keepdims=True)
        hit = blk == first
        sel = jnp.where(hit & (mx > -jnp.inf), 1.0, sel)
        gate = jnp.where(hit, -jnp.inf, gate)
    return sel


def _moba_prompt_kernel(q_ref, k_ref, v_ref, o_ref, kb_scr, vt_scr, kmean_scr, sel_scr,
                        sa_scr, sb_scr, acc_scr, pa_scr, pb_scr, *, nb, n_sel):
    i = pl.program_id(2)
    blk = MOBA_BLOCK
    scale = HEAD_W ** -0.5

    @pl.when(i == 0)
    def _():
        def prep(n, carry):
            rows = pl.ds(pl.multiple_of(n * blk, blk), blk)
            kblk = k_ref[rows, :]
            kb_scr[n] = kblk.astype(BF16)
            kmean_scr[pl.ds(n, 1), :] = jnp.mean(kblk, axis=0, keepdims=True)
            vt_scr[n] = v_ref[rows, :].T.astype(BF16)
            return carry
        lax.fori_loop(0, nb, prep, 0)

    q = q_ref[...]
    qb = q.astype(BF16)
    n_q = q.shape[0]
    second = lax.broadcasted_iota(jnp.int32, (1, n_q), 1) >= blk
    if n_sel > 0:
        gate = lax.dot_general(kmean_scr[...], q, _NT, precision=HIGHEST, preferred_element_type=F32)
        bid = lax.broadcasted_iota(jnp.int32, gate.shape, 0)
        gate = jnp.where(bid < 2 * i + second.astype(jnp.int32), gate, -jnp.inf)
        sel_scr[...] = _top_blocks(gate, n_sel)

    c = scale * 1.4426950408889634

    def scores(n):
        return lax.dot_general(kb_scr[n], qb, _NT, preferred_element_type=F32)

    s0 = scores(2 * i)
    s1 = scores(2 * i + 1)
    kpos = lax.broadcasted_iota(jnp.int32, s0.shape, 0)
    qloc = lax.broadcasted_iota(jnp.int32, s0.shape, 1) % blk
    causal = kpos <= qloc
    picked0 = (sel_scr[pl.ds(2 * i, 1), :] > 0.5) if n_sel > 0 else jnp.zeros((1, n_q), bool)
    s0 = jnp.where((second & picked0) | (jnp.logical_not(second) & causal), s0, NEG_BIG)
    s1 = jnp.where(second & causal, s1, NEG_BIG)
    m = jnp.maximum(jnp.max(s0, axis=0, keepdims=True), jnp.max(s1, axis=0, keepdims=True))
    p0 = jnp.exp2((s0 - m) * c)
    p1 = jnp.exp2((s1 - m) * c)
    l = jnp.sum(p0, axis=0, keepdims=True) + jnp.sum(p1, axis=0, keepdims=True)

    acc_scr[...] = jnp.zeros_like(acc_scr)
    pa_scr[...] = p0.astype(BF16)
    pb_scr[...] = p1.astype(BF16)

    def pv(n, p_ref):
        return jnp.dot(vt_scr[n], p_ref[...], preferred_element_type=F32)

    def softmax_step(s_ref, p_ref, n, m, l):
        picked = sel_scr[pl.ds(n, 1), :] > 0.5
        s = s_ref[...]
        m_new = jnp.where(picked, jnp.maximum(m, jnp.max(s, axis=0, keepdims=True)), m)
        p = jnp.exp2((s - jnp.where(picked, m_new, -NEG_BIG)) * c)
        l = jnp.exp2((m - m_new) * c) * l + jnp.sum(p, axis=0, keepdims=True)
        p_ref[...] = p.astype(BF16)
        return m_new, l

    def body(j, carry):
        m_in, l, ma_prev, na_prev, nb_prev = carry
        n_a = 2 * j
        n_b = n_a + 1
        sb_scr[...] = scores(n_b)
        pva = pv(na_prev, pa_scr)
        pvb = pv(nb_prev, pb_scr)
        m_a, l = softmax_step(sa_scr, pa_scr, n_a, m_in, l)
        sa_scr[...] = scores(jnp.minimum(n_a + 2, nb - 1))
        m_b, l = softmax_step(sb_scr, pb_scr, n_b, m_a, l)
        acc_scr[...] = (jnp.exp2((m_in - m_b) * c) * (acc_scr[...] + pvb)
                        + jnp.exp2((ma_prev - m_b) * c) * pva)
        return m_b, l, m_a, n_a, n_b

    carry = (m, l, m, 2 * i, 2 * i + 1)
    if n_sel > 0:
        @pl.when(i > 0)
        def _():
            sa_scr[...] = scores(0)
        carry = lax.fori_loop(0, i, body, carry)
    m_f, l, ma_last, na_last, nb_last = carry
    acc = acc_scr[...] + pv(nb_last, pb_scr) + jnp.exp2((ma_last - m_f) * c) * pv(na_last, pa_scr)
    o_ref[...] = (acc / l).T.astype(BF16)


def _moba_prompt(main, k, v, *, batch):
    n = main.shape[0]
    seq = n // batch
    nb = seq // MOBA_BLOCK
    n_sel = min(MOBA_TOPK, nb - 1)
    assert nb % 2 == 0, "a query tile is two MoBA blocks"
    nq = nb // 2
    tq = 2 * MOBA_BLOCK
    q_col0 = 4 * HEADS
    return pl.pallas_call(
        functools.partial(_moba_prompt_kernel, nb=nb, n_sel=n_sel),
        grid=(batch, HEADS, nq),
        in_specs=[
            pl.BlockSpec((tq, HEAD_W), lambda b, h, i: (b * nq + i, q_col0 + h)),
            pl.BlockSpec((seq, HEAD_W), lambda b, h, i: (b, h)),
            pl.BlockSpec((seq, HEAD_W), lambda b, h, i: (b, h)),
        ],
        out_specs=pl.BlockSpec((tq, HEAD_W), lambda b, h, i: (b * nq + i, h)),
        out_shape=jax.ShapeDtypeStruct((n, GROUP_W), BF16),
        scratch_shapes=[
            pltpu.VMEM((nb, MOBA_BLOCK, HEAD_W), BF16),
            pltpu.VMEM((nb, HEAD_W, MOBA_BLOCK), BF16),
            pltpu.VMEM((max(nb, 8), HEAD_W), F32),
            pltpu.VMEM((max(nb, 8), tq), F32),
            pltpu.VMEM((MOBA_BLOCK, tq), F32),
            pltpu.VMEM((MOBA_BLOCK, tq), F32),
            pltpu.VMEM((HEAD_W, tq), F32),
            pltpu.VMEM((MOBA_BLOCK, tq), BF16),
            pltpu.VMEM((MOBA_BLOCK, tq), BF16),
        ],
        compiler_params=_params("arbitrary", "arbitrary", "arbitrary"),
        name="moba_prompt",
    )(main, k, v)


def _route(logits):
    lane = lax.broadcasted_iota(jnp.int32, logits.shape, 1)
    is_g = lane < N_GROUPS
    gl = jnp.where(is_g, logits, -jnp.inf)
    gmax = jnp.max(gl, axis=-1, keepdims=True)
    g_sel = jnp.min(jnp.where(gl == gmax, lane, LANES), axis=-1, keepdims=True)
    p_group = 1.0 / jnp.sum(jnp.where(is_g, jnp.exp(gl - gmax), 0.0), axis=-1, keepdims=True)
    e_lo = N_GROUPS + EXPERTS_PER_GROUP * g_sel
    is_e = (lane >= e_lo) & (lane < e_lo + EXPERTS_PER_GROUP)
    el = jnp.where(is_e, logits, -jnp.inf)
    emax = jnp.max(el, axis=-1, keepdims=True)
    ee = jnp.where(is_e, jnp.exp(el - emax), 0.0)
    prob = ee / jnp.sum(ee, axis=-1, keepdims=True)
    prob = jnp.where(is_e, prob, -1.0)
    p1 = jnp.max(prob, axis=-1, keepdims=True)
    i1 = jnp.min(jnp.where(prob == p1, lane, LANES), axis=-1, keepdims=True)
    prob2 = jnp.where(lane == i1, -1.0, prob)
    p2 = jnp.max(prob2, axis=-1, keepdims=True)
    i2 = jnp.min(jnp.where(prob2 == p2, lane, LANES), axis=-1, keepdims=True)
    tot = p1 + p2
    w1 = p_group * (p1 / tot)
    w2 = p_group * (p2 / tot)
    e1 = (i1 - N_GROUPS).astype(F32)
    e2 = (i2 - N_GROUPS).astype(F32)
    return jnp.where(lane == 0, e1, jnp.where(lane == 1, e2, jnp.where(lane == 2, w1, jnp.where(lane == 3, w2, 0.0))))


ROW_TILES = D_MODEL // LANES


def _store_rows(ref, x):
    for j in range(ROW_TILES):
        ref[:, j, :] = x[:, j * LANES:(j + 1) * LANES]


def _load_planes(ref, rows=slice(None)):
    return jnp.concatenate([ref[j, rows, :] for j in range(ROW_TILES)], axis=1)


def _outproj_kernel(x_ref, oh_ref, oa_ref, w_ref, nw_ref, wr_ref, x1_ref, hn_ref, route_ref, *, precise):
    y = _mm(oh_ref[...], w_ref[0:GROUP_W, :], precise=precise)
    y = y + _mm(oa_ref[...], w_ref[GROUP_W:2 * GROUP_W, :], precise=precise)
    x1 = x_ref[...] + y
    x1_ref[...] = x1
    ms = jnp.mean(x1 * x1, axis=-1, keepdims=True)
    hn = x1 * lax.rsqrt(ms + EPS) * nw_ref[...]
    _store_rows(hn_ref, hn)
    if precise:
        logits = jnp.dot(hn, wr_ref[...], precision=HIGHEST, preferred_element_type=F32)
    else:
        logits = _mm3(hn, wr_ref[...])
    route_ref[...] = _route(logits)


def _outproj(x, oh, oa, w_out, norm_w, w_router):
    n = x.shape[0]
    tm = min(n, 512)
    row = lambda w: pl.BlockSpec((tm, w), lambda i: (i, 0))
    full = lambda a, b: pl.BlockSpec((a, b), lambda i: (0, 0), pipeline_mode=pl.Buffered(1))
    return pl.pallas_call(
        functools.partial(_outproj_kernel, precise=w_out.dtype == F32),
        grid=(n // tm,),
        in_specs=[row(D_MODEL), row(GROUP_W), row(GROUP_W), full(2 * GROUP_W, D_MODEL), full(1, D_MODEL),
                  full(D_MODEL, LANES)],
        out_specs=[row(D_MODEL), pl.BlockSpec((tm, ROW_TILES, LANES), lambda i: (i, 0, 0)), row(LANES)],
        out_shape=[jax.ShapeDtypeStruct((n, D_MODEL), F32), jax.ShapeDtypeStruct((n, ROW_TILES, LANES), F32),
                   jax.ShapeDtypeStruct((n, LANES), F32)],
        compiler_params=_params("arbitrary"),
        name="outproj",
    )(x, oh, oa, w_out, norm_w, w_router)


def _moe_kernel(te_ref, tv_ref, rt_ref, hn_hbm, wg_ref, wu_ref, wd_ref, y_ref, xbuf, sem):
    t = pl.program_id(0)
    nt = pl.num_programs(0)
    slot = t % 2
    rows = xbuf.shape[2]

    def row_copy(tile, r, slot):
        tok = rt_ref[tile * rows + r]
        return pltpu.make_async_copy(hn_hbm.at[tok], xbuf.at[slot, :, r, :], sem.at[slot])

    def issue(tile, slot):
        def body(r, carry):
            row_copy(tile, r, slot).start()
            return carry
        lax.fori_loop(0, rows, body, 0, unroll=8)

    @pl.when((t == 0) & (tv_ref[0] > 0))
    def _():
        issue(0, 0)

    @pl.when((t + 1 < nt) & (tv_ref[jnp.minimum(t + 1, nt - 1)] > 0))
    def _():
        issue(t + 1, 1 - slot)

    @pl.when(tv_ref[t] > 0)
    def _():
        def wait_body(r, carry):
            row_copy(t, r, slot).wait()
            return carry
        lax.fori_loop(0, rows, wait_body, 0, unroll=8)
        x = _load_planes(xbuf.at[slot]).astype(BF16)
        a = jnp.dot(x, wg_ref[0], preferred_element_type=F32)
        b = jnp.dot(x, wu_ref[0], preferred_element_type=F32)
        act = (a * jax.nn.sigmoid(a)) * b
        _store_rows(y_ref, jnp.dot(act.astype(BF16), wd_ref[0], preferred_element_type=F32))

    @pl.when(tv_ref[t] == 0)
    def _():
        y_ref[...] = jnp.zeros_like(y_ref)


def _moe(hn, tile_expert, tile_valid, row_token, wg, wu, wd):
    n_rows = row_token.shape[0]
    n_tiles = tile_expert.shape[0]
    rows = n_rows // n_tiles
    grid_spec = pltpu.PrefetchScalarGridSpec(
        num_scalar_prefetch=3,
        grid=(n_tiles,),
        in_specs=[
            pl.BlockSpec(memory_space=pl.ANY),
            pl.BlockSpec((1, D_MODEL, EXPERT_FF), lambda t, te, tv, rt: (te[t], 0, 0)),
            pl.BlockSpec((1, D_MODEL, EXPERT_FF), lambda t, te, tv, rt: (te[t], 0, 0)),
            pl.BlockSpec((1, EXPERT_FF, D_MODEL), lambda t, te, tv, rt: (te[t], 0, 0)),
        ],
        out_specs=pl.BlockSpec((rows, ROW_TILES, LANES), lambda t, te, tv, rt: (t, 0, 0)),
        scratch_shapes=[pltpu.VMEM((2, ROW_TILES, rows, LANES), F32), pltpu.SemaphoreType.DMA((2,))],
    )
    return pl.pallas_call(
        _moe_kernel,
        grid_spec=grid_spec,
        out_shape=jax.ShapeDtypeStruct((n_rows, ROW_TILES, LANES), F32),
        compiler_params=_params("arbitrary"),
        name="moe",
    )(tile_expert, tile_valid, row_token, hn, wg, wu, wd)


def _moe_plan(route, tile):
    n = route.shape[0]
    n_rows = 2 * n + N_EXPERTS * tile
    ids = route[:, 0:2].astype(jnp.int32).reshape(-1)
    onehot = (ids[:, None] == jnp.arange(N_EXPERTS, dtype=jnp.int32)[None, :]).astype(jnp.int32)
    csum = jnp.cumsum(onehot, axis=0)
    rank = jnp.take_along_axis(csum, ids[:, None], axis=1)[:, 0] - 1
    counts = csum[-1]
    padded = ((counts + tile - 1) // tile) * tile
    ends = jnp.cumsum(padded)
    starts = ends - padded
    dest = starts[ids] + rank
    token = jnp.arange(2 * n, dtype=jnp.int32) // 2
    row_token = jnp.zeros((n_rows,), jnp.int32).at[dest].set(token)
    tile_start = jnp.arange(n_rows // tile, dtype=jnp.int32) * tile
    tile_expert = jnp.sum((tile_start[:, None] >= ends[None, :]).astype(jnp.int32), axis=1)
    tile_valid = (tile_expert < N_EXPERTS).astype(jnp.int32)
    last = jnp.max(jnp.where(counts > 0, jnp.arange(N_EXPERTS, dtype=jnp.int32), 0))
    tile_expert = jnp.where(tile_valid > 0, tile_expert, last).astype(jnp.int32)
    return tile_expert, tile_valid, row_token, dest


def _combine_kernel(dest_ref, x1_ref, route_ref, y_hbm, nw_ref, o_ref, ybuf, sem, *, tm):
    t = pl.program_id(0)
    nt = pl.num_programs(0)
    slot = t % 2

    def row_copy(tile, r, slot):
        src = dest_ref[tile * (2 * tm) + r]
        return pltpu.make_async_copy(y_hbm.at[src], ybuf.at[slot, :, r, :], sem.at[slot])

    def issue(tile, slot):
        def body(r, carry):
            row_copy(tile, r, slot).start()
            return carry
        lax.fori_loop(0, 2 * tm, body, 0, unroll=8)

    @pl.when(t == 0)
    def _():
        issue(0, 0)

    @pl.when(t + 1 < nt)
    def _():
        issue(t + 1, 1 - slot)

    def wait_body(r, carry):
        row_copy(t, r, slot).wait()
        return carry
    lax.fori_loop(0, 2 * tm, wait_body, 0, unroll=8)

    route = route_ref[...]
    y0 = _load_planes(ybuf.at[slot], slice(0, tm))
    y1 = _load_planes(ybuf.at[slot], slice(tm, 2 * tm))
    x = x1_ref[...] + (route[:, 2:3] * y0 + route[:, 3:4] * y1)
    ms = jnp.mean(x * x, axis=-1, keepdims=True)
    o_ref[...] = x * lax.rsqrt(ms + EPS) * nw_ref[...]


COMBINE_TILE = 128


def _combine(x1, route, y_sorted, dest_tiled, norm_w):
    n = x1.shape[0]
    tm = min(n, COMBINE_TILE)
    grid_spec = pltpu.PrefetchScalarGridSpec(
        num_scalar_prefetch=1,
        grid=(n // tm,),
        in_specs=[
            pl.BlockSpec((tm, D_MODEL), lambda t, d: (t, 0)),
            pl.BlockSpec((tm, LANES), lambda t, d: (t, 0)),
            pl.BlockSpec(memory_space=pl.ANY),
            pl.BlockSpec((1, D_MODEL), lambda t, d: (0, 0)),
        ],
        out_specs=pl.BlockSpec((tm, D_MODEL), lambda t, d: (t, 0)),
        scratch_shapes=[pltpu.VMEM((2, ROW_TILES, 2 * tm, LANES), F32), pltpu.SemaphoreType.DMA((2,))],
    )
    return pl.pallas_call(
        functools.partial(_combine_kernel, tm=tm),
        grid_spec=grid_spec,
        out_shape=jax.ShapeDtypeStruct((n, D_MODEL), F32),
        compiler_params=_params("arbitrary"),
        name="combine",
    )(dest_tiled, x1, route, y_sorted, norm_w)


def _ffn_and_final(x1, hn, route, wg, wu, wd, norm_final):
    n = x1.shape[0]
    tile = min(MOE_TILE, max(MOE_TILE_MIN, n // 4))
    tile_expert, tile_valid, row_token, dest = _moe_plan(route, tile)
    y_sorted = _moe(hn, tile_expert, tile_valid, row_token, wg, wu, wd)
    tm = min(n, COMBINE_TILE)
    dest_tiled = dest.reshape(n // tm, tm, 2).transpose(0, 2, 1).reshape(-1)
    return _combine(x1, route, y_sorted, dest_tiled, norm_final)


def _kmean_kernel(pt_ref, *refs, pages_per_step):
    page_refs = refs[:pages_per_step]
    o_ref = refs[pages_per_step]
    j = pl.program_id(1)
    ppb = MOBA_BLOCK // PAGE_SIZE
    for blk in range(pages_per_step // ppb):
        tot = jnp.sum(page_refs[blk * ppb][0], axis=0)
        for pg in range(1, ppb):
            tot = tot + jnp.sum(page_refs[blk * ppb + pg][0], axis=0)
        o_ref[0, j * (pages_per_step // ppb) + blk] = tot * (1.0 / MOBA_BLOCK)


def _kmean(cache_k, page_table):
    dec_batch, n_pages = page_table.shape
    ppb = MOBA_BLOCK // PAGE_SIZE
    n_blocks = n_pages // ppb
    pages_per_step = 16
    steps = n_pages // pages_per_step

    def page_spec(p):
        return pl.BlockSpec((1, PAGE_SIZE, HEADS, HEAD_W),
                            lambda s, j, pt, p=p: (pt[s * n_pages + j * pages_per_step + p], 0, 0, 0))

    grid_spec = pltpu.PrefetchScalarGridSpec(
        num_scalar_prefetch=1,
        grid=(dec_batch, steps),
        in_specs=[page_spec(p) for p in range(pages_per_step)],
        out_specs=pl.BlockSpec((1, n_blocks, HEADS, HEAD_W), lambda s, j, pt: (s, 0, 0, 0)),
    )
    return pl.pallas_call(
        functools.partial(_kmean_kernel, pages_per_step=pages_per_step),
        grid_spec=grid_spec,
        out_shape=jax.ShapeDtypeStruct((dec_batch, n_blocks, HEADS, HEAD_W), F32),
        compiler_params=_params("arbitrary", "arbitrary"),
        name="kmean",
    )(page_table.reshape(-1), *([cache_k] * pages_per_step))


def _sample_topk_kernel(q_ref, kmean_ref, idx_ref, q_scr, *, t_new, n_sel):
    q_scr[...] = jnp.zeros_like(q_scr)
    for h in range(HEADS):
        sl = slice(h * HEAD_W, (h + 1) * HEAD_W)
        q_scr[0:t_new, :] = q_ref[0, :, sl]
        gate = lax.dot_general(kmean_ref[0, :, h, :], q_scr[...], _NT, precision=HIGHEST,
                               preferred_element_type=F32)
        nb = gate.shape[0]
        blk = lax.broadcasted_iota(jnp.int32, gate.shape, 0)
        row = lax.broadcasted_iota(jnp.int32, (8, LANES), 0)
        tile = jnp.zeros((8, LANES), jnp.int32)
        for r in range(n_sel):
            mx = jnp.max(gate, axis=0, keepdims=True)
            first = jnp.min(jnp.where(gate == mx, blk, nb), axis=0, keepdims=True)
            tile = jnp.where(row == r, jnp.broadcast_to(first, (8, LANES)), tile)
            gate = jnp.where(blk == first, -jnp.inf, gate)
        idx_ref[0, h] = tile


def _sample_topk(main3, kmean, *, n_sel):
    dec_batch, t_new, _ = main3.shape
    n_blocks = kmean.shape[1]
    return pl.pallas_call(
        functools.partial(_sample_topk_kernel, t_new=t_new, n_sel=n_sel),
        grid=(dec_batch,),
        in_specs=[
            pl.BlockSpec((1, t_new, GROUP_W), lambda s: (s, 0, 4)),
            pl.BlockSpec((1, n_blocks, HEADS, HEAD_W), lambda s: (s, 0, 0, 0)),
        ],
        out_specs=pl.BlockSpec((1, HEADS, 8, LANES), lambda s: (s, 0, 0, 0)),
        out_shape=jax.ShapeDtypeStruct((dec_batch, HEADS, 8, LANES), jnp.int32),
        scratch_shapes=[pltpu.VMEM((LANES, HEAD_W), F32)],
        compiler_params=_params("arbitrary"),
        name="sample_topk",
    )(main3, kmean)


def _sample_attn_kernel(idx_ref, pt_ref, q_ref, kn_ref, vn_ref, ck_hbm, cv_hbm, o_ref,
                        kbuf, vbuf, q_scr, kn_scr, vn_scr, sem, *, t_new, n_sel, n_pages):
    s = pl.program_id(0)
    h = pl.program_id(1)
    step = s * HEADS + h
    n_steps = pl.num_programs(0) * HEADS
    slot = step % 2
    ppb = MOBA_BLOCK // PAGE_SIZE
    scale = HEAD_W ** -0.5

    def copies(step, slot):
        s_ = step // HEADS
        h_ = step % HEADS
        out = []
        for t in range(t_new):
            for r in range(n_sel):
                b = idx_ref[(step * t_new + t) * n_sel + r]
                for pg in range(ppb):
                    page = pt_ref[s_ * n_pages + b * ppb + pg]
                    rows = pl.ds((r * ppb + pg) * PAGE_SIZE, PAGE_SIZE)
                    out.append(pltpu.make_async_copy(ck_hbm.at[page, :, h_, :], kbuf.at[slot, t, rows, :],
                                                     sem.at[slot]))
                    out.append(pltpu.make_async_copy(cv_hbm.at[page, :, h_, :], vbuf.at[slot, t, rows, :],
                                                     sem.at[slot]))
        return out

    @pl.when(step == 0)
    def _():
        for c in copies(0, 0):
            c.start()

    @pl.when(step + 1 < n_steps)
    def _():
        for c in copies(step + 1, 1 - slot):
            c.start()

    q_scr[...] = jnp.zeros_like(q_scr)
    kn_scr[...] = jnp.zeros_like(kn_scr)
    vn_scr[...] = jnp.zeros_like(vn_scr)
    q_scr[0:t_new, :] = q_ref[0]
    kn_scr[0:t_new, :] = kn_ref[0]
    vn_scr[0:t_new, :] = vn_ref[0]
    mm = _mm3
    qb = q_scr[...]
    s_new = mm(qb, kn_scr[...], _NT) * scale
    trow = lax.broadcasted_iota(jnp.int32, s_new.shape, 0)
    jcol = lax.broadcasted_iota(jnp.int32, s_new.shape, 1)
    new_ok = (jcol <= trow) & (jcol < t_new)
    s_new = jnp.where(new_ok, s_new, NEG_BIG)
    m_new = jnp.max(s_new, axis=-1, keepdims=True)

    for c in copies(step, slot):
        c.wait()

    n_keys = n_sel * MOBA_BLOCK
    st = mm(qb, kbuf[slot].reshape(t_new * n_keys, HEAD_W), _NT) * scale
    srow = lax.broadcasted_iota(jnp.int32, st.shape, 0)
    scol = lax.broadcasted_iota(jnp.int32, st.shape, 1)
    own = (scol >= srow * n_keys) & (scol < (srow + 1) * n_keys)
    st = jnp.where(own, st, NEG_BIG)
    m = jnp.maximum(jnp.max(st, axis=-1, keepdims=True), m_new)
    p = jnp.exp(st - m)
    pn = jnp.where(new_ok, jnp.exp(s_new - m), 0.0)
    den = jnp.sum(p, axis=-1, keepdims=True) + jnp.sum(pn, axis=-1, keepdims=True)
    out = (mm(p, vbuf[slot].reshape(t_new * n_keys, HEAD_W)) + mm(pn, vn_scr[...])) / den
    o_ref[0] = out[0:t_new, :]


def _sample_attn(idx, page_table, main3, kn3, vn3, cache_k4, cache_v4, *, n_sel):
    dec_batch, t_new, _ = main3.shape
    n_pages = page_table.shape[1]
    q_col0 = 4 * HEADS
    grid_spec = pltpu.PrefetchScalarGridSpec(
        num_scalar_prefetch=2,
        grid=(dec_batch, HEADS),
        in_specs=[
            pl.BlockSpec((1, t_new, HEAD_W), lambda s, h, ix, pt: (s, 0, q_col0 + h)),
            pl.BlockSpec((1, t_new, HEAD_W), lambda s, h, ix, pt: (s, 0, h)),
            pl.BlockSpec((1, t_new, HEAD_W), lambda s, h, ix, pt: (s, 0, h)),
            pl.BlockSpec(memory_space=pl.ANY),
            pl.BlockSpec(memory_space=pl.ANY),
        ],
        out_specs=pl.BlockSpec((1, t_new, HEAD_W), lambda s, h, ix, pt: (s, 0, h)),
        scratch_shapes=[
            pltpu.VMEM((2, t_new, n_sel * MOBA_BLOCK, HEAD_W), F32),
            pltpu.VMEM((2, t_new, n_sel * MOBA_BLOCK, HEAD_W), F32),
            pltpu.VMEM((8, HEAD_W), F32),
            pltpu.VMEM((16, HEAD_W), F32),
            pltpu.VMEM((16, HEAD_W), F32),
            pltpu.SemaphoreType.DMA((2,)),
        ],
    )
    return pl.pallas_call(
        functools.partial(_sample_attn_kernel, t_new=t_new, n_sel=n_sel, n_pages=n_pages),
        grid_spec=grid_spec,
        out_shape=jax.ShapeDtypeStruct((dec_batch, t_new, GROUP_W), F32),
        compiler_params=_params("arbitrary", "arbitrary"),
        name="sample_attn",
    )(idx, page_table.reshape(-1), main3, kn3, vn3, cache_k4, cache_v4)


def _moba_sample(main, kn, vn, cache_k, cache_v, page_table):
    dec_batch, n_pages = page_table.shape
    t_new = main.shape[0] // dec_batch
    n_blocks = n_pages * PAGE_SIZE // MOBA_BLOCK
    n_sel = min(MOBA_TOPK, n_blocks)
    main3 = main.reshape(dec_batch, t_new, MAIN_W)
    kn3 = kn.reshape(dec_batch, t_new, GROUP_W)
    vn3 = vn.reshape(dec_batch, t_new, GROUP_W)
    kmean = _kmean(cache_k, page_table)
    idx = _sample_topk(main3, kmean, n_sel=n_sel)
    idx = idx[:, :, :n_sel, :t_new].transpose(0, 1, 3, 2).reshape(-1)
    o = _sample_attn(idx, page_table, main3, kn3, vn3, cache_k, cache_v, n_sel=n_sel)
    return o.reshape(dec_batch * t_new, GROUP_W)


def _layer(x, attend, hgrn_fn, weights):
    norm_mix, w_in, lbv, hgrn_norm, w_out, norm_ffn, w_router, wg, wu, wd, norm_final = weights
    main, k, v = _inproj(x, norm_mix, w_in)
    oh, state = hgrn_fn(main, lbv, hgrn_norm)
    oa = attend(main, k, v)
    x1, hn, route = _outproj(x, oh, oa, w_out, norm_ffn, w_router)
    y = _ffn_and_final(x1, hn, route, wg, wu, wd, norm_final)
    return y, k, v, state


def kernel(x_prompt, x_sample, cache_k, cache_v, state_hgrn, page_table, norm_mix, w_in, lb_logits, hgrn_norm,
           w_out, norm_ffn, w_group, w_expert, w_gate, w_up, w_down, norm_final):
    depth = w_in.shape[0]
    assert depth == 1, "the final norm is fused into the (single) layer"
    batch, seq, _ = x_prompt.shape
    dec_batch, t_new, _ = x_sample.shape
    lb_all = jnp.cumsum(jax.nn.softmax(lb_logits.astype(F32), axis=0), axis=0)
    layer = 0
    w_router = jnp.concatenate(
        [w_group[layer], w_expert[layer], jnp.zeros((D_MODEL, LANES - N_GROUPS - N_EXPERTS), F32)], axis=1)
    def weights(mix_dtype):
        return (
            norm_mix[layer][None, :], w_in[layer].astype(mix_dtype), lb_all[layer][None, :],
            hgrn_norm[layer][None, :], w_out[layer].astype(mix_dtype), norm_ffn[layer][None, :], w_router,
            w_gate[layer].reshape(N_EXPERTS, D_MODEL, EXPERT_FF).astype(BF16),
            w_up[layer].reshape(N_EXPERTS, D_MODEL, EXPERT_FF).astype(BF16),
            w_down[layer].reshape(N_EXPERTS, EXPERT_FF, D_MODEL).astype(BF16),
            norm_final[None, :],
        )

    hgrn_p = functools.partial(_hgrn, s0=None, batch=batch, chunk=HGRN_CHUNK, sub=HGRN_SUB, t_valid=HGRN_CHUNK,
                               precise=False)
    moba_p = functools.partial(_moba_prompt, batch=batch)
    yp, kp, vp, sp = _layer(x_prompt.reshape(batch * seq, D_MODEL), moba_p, hgrn_p, weights(BF16))

    pad = 8

    def hgrn_s(main, lbv, nw):
        m3 = main.reshape(dec_batch, t_new, MAIN_W)
        m3 = jnp.pad(m3, ((0, 0), (0, pad - t_new), (0, 0))).reshape(dec_batch * pad, MAIN_W)
        o, st = _hgrn(m3, lbv, nw, state_hgrn[layer], batch=dec_batch, chunk=pad, sub=pad, t_valid=t_new,
                      precise=True)
        return o.reshape(dec_batch, pad, GROUP_W)[:, :t_new].reshape(dec_batch * t_new, GROUP_W), st

    moba_s = functools.partial(_moba_sample, cache_k=cache_k.reshape(cache_k.shape[1:]),
                               cache_v=cache_v.reshape(cache_v.shape[1:]), page_table=page_table)
    ys, ks, vs, ss = _layer(x_sample.reshape(dec_batch * t_new, D_MODEL), moba_s, hgrn_s, weights(F32))

    return (yp.reshape(batch, seq, D_MODEL), ys.reshape(dec_batch, t_new, D_MODEL),
            kp.reshape(1, batch, seq, HEADS, HEAD_W), vp.reshape(1, batch, seq, HEADS, HEAD_W), sp[None],
            ks.reshape(1, dec_batch, t_new, HEADS, HEAD_W), vs.reshape(1, dec_batch, t_new, HEADS, HEAD_W), ss[None])
```

```python
import functools

import jax
import jax.numpy as jnp
from jax import lax
from jax.experimental import pallas as pl
from jax.experimental.pallas import tpu as pltpu

F32 = jnp.float32
BF16 = jnp.bfloat16
HIGHEST = lax.Precision.HIGHEST

D_MODEL = 2048
HEADS = 8
HEAD_W = 128
GROUP_W = HEADS * HEAD_W
MAIN_W = 5 * GROUP_W
HGRN_CHUNK = 64
HGRN_SUB = 16
MOBA_BLOCK = 256
MOBA_TOPK = 3
PAGE_SIZE = 128
N_GROUPS = 4
EXPERTS_PER_GROUP = 8
N_EXPERTS = N_GROUPS * EXPERTS_PER_GROUP
EXPERT_FF = 256
EPS = 1e-6
LANES = 128
MOE_TILE = 256
DMA_GROUP = 8
MOE_TILE_MIN = 32
VMEM_LIMIT = 52 * 1024 * 1024
NEG_BIG = -1e30

_NT = (((1,), (1,)), ((), ()))
_TN = (((0,), (0,)), ((), ()))


def _params(*sem):
    return pltpu.CompilerParams(dimension_semantics=sem, vmem_limit_bytes=VMEM_LIMIT)


def _mm(a, b, dims=None, *, precise):
    if dims is None:
        dims = (((a.ndim - 1,), (0,)), ((), ()))
    if precise:
        return lax.dot_general(a.astype(F32), b.astype(F32), dims, precision=HIGHEST, preferred_element_type=F32)
    return lax.dot_general(a.astype(BF16), b.astype(BF16), dims, preferred_element_type=F32)


def _mm3(a, b, dims=None):
    if dims is None:
        dims = (((a.ndim - 1,), (0,)), ((), ()))

    def halves(x):
        hi = x.astype(BF16)
        return hi, (x - hi.astype(F32)).astype(BF16)

    (ah, al), (bh, bl) = halves(a), halves(b)
    dot = lambda x, y: lax.dot_general(x, y, dims, preferred_element_type=F32)
    return dot(ah, bh) + (dot(ah, bl) + dot(al, bh))


def _inproj_kernel(x_ref, nw_ref, w_ref, main_ref, k_ref, v_ref, h_scr, *, n_main, n_kv, precise):
    j = pl.program_id(1)

    @pl.when(j == 0)
    def _():
        x = x_ref[...]
        ms = jnp.mean(x * x, axis=-1, keepdims=True)
        h_scr[...] = (x * lax.rsqrt(ms + EPS) * nw_ref[...]).astype(h_scr.dtype)

    acc = _mm(h_scr[...], w_ref[...], precise=precise)

    @pl.when(j < n_main)
    def _():
        main_ref[...] = acc

    @pl.when((j >= n_main) & (j < n_main + n_kv))
    def _():
        k_ref[...] = acc

    @pl.when(j >= n_main + n_kv)
    def _():
        v_ref[...] = acc


def _inproj(x, norm_w, w_in):
    n = x.shape[0]
    precise = w_in.dtype == F32
    tm = min(n, 1024)
    tn = 512
    n_main = MAIN_W // tn
    n_kv = GROUP_W // tn
    grid = (n // tm, n_main + 2 * n_kv)
    return pl.pallas_call(
        functools.partial(_inproj_kernel, n_main=n_main, n_kv=n_kv, precise=precise),
        grid=grid,
        in_specs=[
            pl.BlockSpec((tm, D_MODEL), lambda i, j: (i, 0)),
            pl.BlockSpec((1, D_MODEL), lambda i, j: (0, 0)),
            pl.BlockSpec((D_MODEL, tn), lambda i, j: (0, j)),
        ],
        out_specs=[
            pl.BlockSpec((tm, tn), lambda i, j: (i, jnp.minimum(j, n_main - 1))),
            pl.BlockSpec((tm, tn), lambda i, j: (i, jnp.clip(j - n_main, 0, n_kv - 1))),
            pl.BlockSpec((tm, tn), lambda i, j: (i, jnp.clip(j - n_main - n_kv, 0, n_kv - 1))),
        ],
        out_shape=[
            jax.ShapeDtypeStruct((n, MAIN_W), F32),
            jax.ShapeDtypeStruct((n, GROUP_W), F32),
            jax.ShapeDtypeStruct((n, GROUP_W), F32),
        ],
        scratch_shapes=[pltpu.VMEM((tm, D_MODEL), w_in.dtype)],
        compiler_params=_params("arbitrary", "arbitrary"),
        name="inproj",
    )(x, norm_w, w_in)


def _head(x, h):
    return x[:, h * HEAD_W:(h + 1) * HEAD_W]


def _hgrn_chunk(hq, hf, hi, lbv, states, a_scr, *, chunk, sub, t_valid, precise):
    mm = functools.partial(_mm, precise=precise)
    cast = (lambda x: x) if precise else (lambda x: x.astype(BF16))
    g = jnp.log(lbv + (1.0 - lbv) * jax.nn.sigmoid(hf))
    k = (1.0 - lbv) * jax.nn.sigmoid(-hf)
    q = hq * jax.nn.sigmoid(hq)
    if t_valid < chunk:
        live = lax.broadcasted_iota(jnp.int32, g.shape, 0) < t_valid
        g = jnp.where(live, g, 0.0)
        k = jnp.where(live, k, 0.0)
    row = lax.broadcasted_iota(jnp.int32, (chunk, chunk), 0)
    col = lax.broadcasted_iota(jnp.int32, (chunk, chunk), 1)
    tri = (col <= row).astype(BF16)
    g1 = g.astype(BF16)
    r1 = g - g1.astype(F32)
    g2 = r1.astype(BF16)
    g3 = (r1 - g2.astype(F32)).astype(BF16)
    tdot = lambda x: jnp.dot(tri, x, preferred_element_type=F32)
    b = tdot(g1) + (tdot(g2) + tdot(g3))
    vb = cast(hi)
    krow = lax.broadcasted_iota(jnp.int32, (chunk, GROUP_W), 0)
    for blk in range(chunk // sub):
        r0, r1 = blk * sub, (blk + 1) * sub
        beta = b[r0 - 1:r0] if blk > 0 else jnp.zeros((1, GROUP_W), F32)
        qt = cast(q[r0:r1] * jnp.exp(b[r0:r1] - beta))
        kt = cast(k * jnp.exp(jnp.where(krow < r1, beta - b, -jnp.inf)))
        trow = lax.broadcasted_iota(jnp.int32, (sub, chunk), 0) + r0
        scol = lax.broadcasted_iota(jnp.int32, (sub, chunk), 1)
        for h in range(HEADS):
            a_scr[h, r0:r1, :] = jnp.where(scol <= trow, mm(_head(qt, h), _head(kt, h), _NT), 0.0)
    qh = cast(q * jnp.exp(b))
    b_last = b[chunk - 1:chunk]
    kh = cast(k * jnp.exp(b_last - b))
    decay = jnp.exp(b_last)
    outs, new_states = [], []
    for h in range(HEADS):
        outs.append(mm(a_scr[h], _head(vb, h)) + mm(_head(qh, h), states[h], _NT))
        new_states.append(states[h] * _head(decay, h) + mm(_head(vb, h), _head(kh, h), _TN))
    return outs, new_states


def _hgrn_kernel(*refs, chunk, sub, t_valid, has_s0, precise):
    st_scr = refs[-HEADS:]
    a_scr = refs[-HEADS - 1]
    refs = refs[:-HEADS - 1]
    if has_s0:
        q_ref, f_ref, i_ref, g_ref, lb_ref, nw_ref, s0_ref, o_ref, s_ref = refs
    else:
        q_ref, f_ref, i_ref, g_ref, lb_ref, nw_ref, o_ref, s_ref = refs
        s0_ref = None
    c = pl.program_id(1)

    @pl.when(c == 0)
    def _():
        for h in range(HEADS):
            st_scr[h][...] = s0_ref[0, h].T if has_s0 else jnp.zeros((HEAD_W, HEAD_W), F32)

    outs, new_states = _hgrn_chunk(q_ref[...], f_ref[...], i_ref[...], lb_ref[...], [r[...] for r in st_scr],
                                   a_scr, chunk=chunk, sub=sub, t_valid=t_valid, precise=precise)
    normed = []
    for h in range(HEADS):
        st_scr[h][...] = new_states[h]
        o = outs[h]
        normed.append(o * lax.rsqrt(jnp.mean(o * o, axis=-1, keepdims=True) + EPS))
    hg = g_ref[...]
    o_all = jnp.concatenate(normed, axis=1) * nw_ref[...] * (hg * jax.nn.sigmoid(hg))
    o_ref[...] = o_all.astype(o_ref.dtype)

    @pl.when(c == pl.num_programs(1) - 1)
    def _():
        for h in range(HEADS):
            s_ref[0, h] = st_scr[h][...].T


def _hgrn(main, lbv, norm_w, s0, *, batch, chunk, sub, t_valid, precise):
    n = main.shape[0]
    n_chunks = n // batch // chunk
    slab = lambda s: pl.BlockSpec((chunk, GROUP_W), lambda b, c, s=s: (b * n_chunks + c, s))
    vec = pl.BlockSpec((1, GROUP_W), lambda b, c: (0, 0))
    state = pl.BlockSpec((1, HEADS, HEAD_W, HEAD_W), lambda b, c: (b, 0, 0, 0))
    in_specs = [slab(0), slab(1), slab(2), slab(3), vec, vec]
    args = [main, main, main, main, lbv, norm_w]
    if s0 is not None:
        in_specs.append(state)
        args.append(s0)
    return pl.pallas_call(
        functools.partial(_hgrn_kernel, chunk=chunk, sub=sub, t_valid=t_valid, has_s0=s0 is not None,
                          precise=precise),
        grid=(batch, n_chunks),
        in_specs=in_specs,
        out_specs=[pl.BlockSpec((chunk, GROUP_W), lambda b, c: (b * n_chunks + c, 0)), state],
        out_shape=[jax.ShapeDtypeStruct((n, GROUP_W), F32 if precise else BF16),
                   jax.ShapeDtypeStruct((batch, HEADS, HEAD_W, HEAD_W), F32)],
        scratch_shapes=([pltpu.VMEM((HEADS, chunk, chunk), F32)]
                        + [pltpu.VMEM((HEAD_W, HEAD_W), F32) for _ in range(HEADS)]),
        compiler_params=_params("arbitrary", "arbitrary"),
        name="hgrn",
    )(*args)


def _top_blocks(gate, n_sel):
    nb = gate.shape[0]
    blk = lax.broadcasted_iota(jnp.int32, gate.shape, 0)
    sel = jnp.zeros(gate.shape, F32)
    for _ in range(n_sel):
        mx = jnp.max(gate, axis=0, keepdims=True)
        first = jnp.min(jnp.where(gate == mx, blk, nb), axis=0, keepdims=True)
        hit = blk == first
        sel = jnp.where(hit & (mx > -jnp.inf), 1.0, sel)
        gate = jnp.where(hit, -jnp.inf, gate)
    return sel


def _moba_prompt_kernel(q_ref, k_ref, v_ref, o_ref, kb_scr, vt_scr, kmean_scr, sel_scr,
                        sa_scr, sb_scr, acc_scr, pa_scr, pb_scr, *, nb, n_sel):
    i = pl.program_id(2)
    blk = MOBA_BLOCK
    scale = HEAD_W ** -0.5

    @pl.when(i == 0)
    def _():
        def prep(n, carry):
            rows = pl.ds(pl.multiple_of(n * blk, blk), blk)
            kblk = k_ref[rows, :]
            kb_scr[n] = kblk.astype(BF16)
            kmean_scr[pl.ds(n, 1), :] = jnp.mean(kblk, axis=0, keepdims=True)
            vt_scr[n] = v_ref[rows, :].T.astype(BF16)
            return carry
        lax.fori_loop(0, nb, prep, 0)

    q = q_ref[...]
    qb = q.astype(BF16)
    n_q = q.shape[0]
    second = lax.broadcasted_iota(jnp.int32, (1, n_q), 1) >= blk
    if n_sel > 0:
        gate = lax.dot_general(kmean_scr[...], q, _NT, precision=HIGHEST, preferred_element_type=F32)
        bid = lax.broadcasted_iota(jnp.int32, gate.shape, 0)
        gate = jnp.where(bid < 2 * i + second.astype(jnp.int32), gate, -jnp.inf)
        sel_scr[...] = _top_blocks(gate, n_sel)

    c = scale * 1.4426950408889634

    def scores(n):
        return lax.dot_general(kb_scr[n], qb, _NT, preferred_element_type=F32)

    s0 = scores(2 * i)
    s1 = scores(2 * i + 1)
    kpos = lax.broadcasted_iota(jnp.int32, s0.shape, 0)
    qloc = lax.broadcasted_iota(jnp.int32, s0.shape, 1) % blk
    causal = kpos <= qloc
    picked0 = (sel_scr[pl.ds(2 * i, 1), :] > 0.5) if n_sel > 0 else jnp.zeros((1, n_q), bool)
    s0 = jnp.where((second & picked0) | (jnp.logical_not(second) & causal), s0, NEG_BIG)
    s1 = jnp.where(second & causal, s1, NEG_BIG)
    m = jnp.maximum(jnp.max(s0, axis=0, keepdims=True), jnp.max(s1, axis=0, keepdims=True))
    p0 = jnp.exp2((s0 - m) * c)
    p1 = jnp.exp2((s1 - m) * c)
    l = jnp.sum(p0, axis=0, keepdims=True) + jnp.sum(p1, axis=0, keepdims=True)

    acc_scr[...] = jnp.zeros_like(acc_scr)
    pa_scr[...] = p0.astype(BF16)
    pb_scr[...] = p1.astype(BF16)

    def pv(n, p_ref):
        return jnp.dot(vt_scr[n], p_ref[...], preferred_element_type=F32)

    def softmax_step(s_ref, p_ref, n, m, l):
        picked = sel_scr[pl.ds(n, 1), :] > 0.5
        s = s_ref[...]
        m_new = jnp.where(picked, jnp.maximum(m, jnp.max(s, axis=0, keepdims=True)), m)
        p = jnp.exp2((s - jnp.where(picked, m_new, -NEG_BIG)) * c)
        l = jnp.exp2((m - m_new) * c) * l + jnp.sum(p, axis=0, keepdims=True)
        p_ref[...] = p.astype(BF16)
        return m_new, l

    def body(j, carry):
        m_in, l, ma_prev, na_prev, nb_prev = carry
        n_a = 2 * j
        n_b = n_a + 1
        sb_scr[...] = scores(n_b)
        pva = pv(na_prev, pa_scr)
        pvb = pv(nb_prev, pb_scr)
        m_a, l = softmax_step(sa_scr, pa_scr, n_a, m_in, l)
        sa_scr[...] = scores(jnp.minimum(n_a + 2, nb - 1))
        m_b, l = softmax_step(sb_scr, pb_scr, n_b, m_a, l)
        acc_scr[...] = (jnp.exp2((m_in - m_b) * c) * (acc_scr[...] + pvb)
                        + jnp.exp2((ma_prev - m_b) * c) * pva)
        return m_b, l, m_a, n_a, n_b

    carry = (m, l, m, 2 * i, 2 * i + 1)
    if n_sel > 0:
        @pl.when(i > 0)
        def _():
            sa_scr[...] = scores(0)
        carry = lax.fori_loop(0, i, body, carry)
    m_f, l, ma_last, na_last, nb_last = carry
    acc = acc_scr[...] + pv(nb_last, pb_scr) + jnp.exp2((ma_last - m_f) * c) * pv(na_last, pa_scr)
    o_ref[...] = (acc / l).T.astype(BF16)


def _moba_prompt(main, k, v, *, batch):
    n = main.shape[0]
    seq = n // batch
    nb = seq // MOBA_BLOCK
    n_sel = min(MOBA_TOPK, nb - 1)
    assert nb % 2 == 0, "a query tile is two MoBA blocks"
    nq = nb // 2
    tq = 2 * MOBA_BLOCK
    q_col0 = 4 * HEADS
    return pl.pallas_call(
        functools.partial(_moba_prompt_kernel, nb=nb, n_sel=n_sel),
        grid=(batch, HEADS, nq),
        in_specs=[
            pl.BlockSpec((tq, HEAD_W), lambda b, h, i: (b * nq + i, q_col0 + h)),
            pl.BlockSpec((seq, HEAD_W), lambda b, h, i: (b, h)),
            pl.BlockSpec((seq, HEAD_W), lambda b, h, i: (b, h)),
        ],
        out_specs=pl.BlockSpec((tq, HEAD_W), lambda b, h, i: (b * nq + i, h)),
        out_shape=jax.ShapeDtypeStruct((n, GROUP_W), BF16),
        scratch_shapes=[
            pltpu.VMEM((nb, MOBA_BLOCK, HEAD_W), BF16),
            pltpu.VMEM((nb, HEAD_W, MOBA_BLOCK), BF16),
            pltpu.VMEM((max(nb, 8), HEAD_W), F32),
            pltpu.VMEM((max(nb, 8), tq), F32),
            pltpu.VMEM((MOBA_BLOCK, tq), F32),
            pltpu.VMEM((MOBA_BLOCK, tq), F32),
            pltpu.VMEM((HEAD_W, tq), F32),
            pltpu.VMEM((MOBA_BLOCK, tq), BF16),
            pltpu.VMEM((MOBA_BLOCK, tq), BF16),
        ],
        compiler_params=_params("arbitrary", "arbitrary", "arbitrary"),
        name="moba_prompt",
    )(main, k, v)


def _route(logits):
    lane = lax.broadcasted_iota(jnp.int32, logits.shape, 1)
    is_g = lane < N_GROUPS
    gl = jnp.where(is_g, logits, -jnp.inf)
    gmax = jnp.max(gl, axis=-1, keepdims=True)
    g_sel = jnp.min(jnp.where(gl == gmax, lane, LANES), axis=-1, keepdims=True)
    p_group = 1.0 / jnp.sum(jnp.where(is_g, jnp.exp(gl - gmax), 0.0), axis=-1, keepdims=True)
    e_lo = N_GROUPS + EXPERTS_PER_GROUP * g_sel
    is_e = (lane >= e_lo) & (lane < e_lo + EXPERTS_PER_GROUP)
    el = jnp.where(is_e, logits, -jnp.inf)
    emax = jnp.max(el, axis=-1, keepdims=True)
    ee = jnp.where(is_e, jnp.exp(el - emax), 0.0)
    prob = ee / jnp.sum(ee, axis=-1, keepdims=True)
    prob = jnp.where(is_e, prob, -1.0)
    p1 = jnp.max(prob, axis=-1, keepdims=True)
    i1 = jnp.min(jnp.where(prob == p1, lane, LANES), axis=-1, keepdims=True)
    prob2 = jnp.where(lane == i1, -1.0, prob)
    p2 = jnp.max(prob2, axis=-1, keepdims=True)
    i2 = jnp.min(jnp.where(prob2 == p2, lane, LANES), axis=-1, keepdims=True)
    tot = p1 + p2
    w1 = p_group * (p1 / tot)
    w2 = p_group * (p2 / tot)
    e1 = (i1 - N_GROUPS).astype(F32)
    e2 = (i2 - N_GROUPS).astype(F32)
    return jnp.where(lane == 0, e1, jnp.where(lane == 1, e2, jnp.where(lane == 2, w1, jnp.where(lane == 3, w2, 0.0))))


ROW_TILES = D_MODEL // LANES


def _store_rows(ref, x):
    for j in range(ROW_TILES):
        ref[:, j, :] = x[:, j * LANES:(j + 1) * LANES]


def _load_planes(ref, rows=slice(None)):
    return jnp.concatenate([ref[j, rows, :] for j in range(ROW_TILES)], axis=1)


def _outproj_kernel(x_ref, oh_ref, oa_ref, w_ref, nw_ref, wr_ref, x1_ref, hn_ref, route_ref, *, precise):
    y = _mm(oh_ref[...], w_ref[0:GROUP_W, :], precise=precise)
    y = y + _mm(oa_ref[...], w_ref[GROUP_W:2 * GROUP_W, :], precise=precise)
    x1 = x_ref[...] + y
    x1_ref[...] = x1
    ms = jnp.mean(x1 * x1, axis=-1, keepdims=True)
    hn = x1 * lax.rsqrt(ms + EPS) * nw_ref[...]
    _store_rows(hn_ref, hn)
    if precise:
        logits = jnp.dot(hn, wr_ref[...], precision=HIGHEST, preferred_element_type=F32)
    else:
        logits = _mm3(hn, wr_ref[...])
    route_ref[...] = _route(logits)


def _outproj(x, oh, oa, w_out, norm_w, w_router):
    n = x.shape[0]
    tm = min(n, 512)
    row = lambda w: pl.BlockSpec((tm, w), lambda i: (i, 0))
    full = lambda a, b: pl.BlockSpec((a, b), lambda i: (0, 0), pipeline_mode=pl.Buffered(1))
    return pl.pallas_call(
        functools.partial(_outproj_kernel, precise=w_out.dtype == F32),
        grid=(n // tm,),
        in_specs=[row(D_MODEL), row(GROUP_W), row(GROUP_W), full(2 * GROUP_W, D_MODEL), full(1, D_MODEL),
                  full(D_MODEL, LANES)],
        out_specs=[row(D_MODEL), pl.BlockSpec((tm, ROW_TILES, LANES), lambda i: (i, 0, 0)), row(LANES)],
        out_shape=[jax.ShapeDtypeStruct((n, D_MODEL), F32), jax.ShapeDtypeStruct((n, ROW_TILES, LANES), F32),
                   jax.ShapeDtypeStruct((n, LANES), F32)],
        compiler_params=_params("arbitrary"),
        name="outproj",
    )(x, oh, oa, w_out, norm_w, w_router)


def _moe_kernel(te_ref, tv_ref, rt_ref, hn_hbm, wg_ref, wu_ref, wd_ref, y_ref, xbuf, sem):
    t = pl.program_id(0)
    nt = pl.num_programs(0)
    slot = t % 2
    rows = xbuf.shape[2]

    def row_copy(tile, r, slot):
        tok = rt_ref[tile * rows + r]
        return pltpu.make_async_copy(hn_hbm.at[tok], xbuf.at[slot, :, r, :], sem.at[slot])

    def groups(tile):
        return (tv_ref[tile] + DMA_GROUP - 1) // DMA_GROUP

    def issue(tile, slot):
        def body(g, carry):
            for u in range(DMA_GROUP):
                row_copy(tile, g * DMA_GROUP + u, slot).start()
            return carry
        lax.fori_loop(0, groups(tile), body, 0)

    @pl.when(t == 0)
    def _():
        xbuf[...] = jnp.zeros_like(xbuf)
        issue(0, 0)

    @pl.when(t + 1 < nt)
    def _():
        issue(jnp.minimum(t + 1, nt - 1), 1 - slot)

    @pl.when(tv_ref[t] > 0)
    def _():
        def wait_body(g, carry):
            for u in range(DMA_GROUP):
                row_copy(t, g * DMA_GROUP + u, slot).wait()
            return carry
        lax.fori_loop(0, groups(t), wait_body, 0)
        x = _load_planes(xbuf.at[slot]).astype(BF16)
        a = jnp.dot(x, wg_ref[0], preferred_element_type=F32)
        b = jnp.dot(x, wu_ref[0], preferred_element_type=F32)
        act = (a * jax.nn.sigmoid(a)) * b
        _store_rows(y_ref, jnp.dot(act.astype(BF16), wd_ref[0], preferred_element_type=F32))

    @pl.when(tv_ref[t] == 0)
    def _():
        y_ref[...] = jnp.zeros_like(y_ref)


def _moe(hn, tile_expert, tile_valid, row_token, wg, wu, wd):
    n_rows = row_token.shape[0]
    n_tiles = tile_expert.shape[0]
    rows = n_rows // n_tiles
    grid_spec = pltpu.PrefetchScalarGridSpec(
        num_scalar_prefetch=3,
        grid=(n_tiles,),
        in_specs=[
            pl.BlockSpec(memory_space=pl.ANY),
            pl.BlockSpec((1, D_MODEL, EXPERT_FF), lambda t, te, tv, rt: (te[t], 0, 0)),
            pl.BlockSpec((1, D_MODEL, EXPERT_FF), lambda t, te, tv, rt: (te[t], 0, 0)),
            pl.BlockSpec((1, EXPERT_FF, D_MODEL), lambda t, te, tv, rt: (te[t], 0, 0)),
        ],
        out_specs=pl.BlockSpec((rows, ROW_TILES, LANES), lambda t, te, tv, rt: (t, 0, 0)),
        scratch_shapes=[pltpu.VMEM((2, ROW_TILES, rows, LANES), F32), pltpu.SemaphoreType.DMA((2,))],
    )
    return pl.pallas_call(
        _moe_kernel,
        grid_spec=grid_spec,
        out_shape=jax.ShapeDtypeStruct((n_rows, ROW_TILES, LANES), F32),
        compiler_params=_params("arbitrary"),
        name="moe",
    )(tile_expert, tile_valid, row_token, hn, wg, wu, wd)


def _moe_plan(route, tile):
    n = route.shape[0]
    n_rows = 2 * n + N_EXPERTS * tile
    ids = route[:, 0:2].astype(jnp.int32).reshape(-1)
    onehot = (ids[:, None] == jnp.arange(N_EXPERTS, dtype=jnp.int32)[None, :]).astype(jnp.int32)
    csum = jnp.cumsum(onehot, axis=0)
    rank = jnp.take_along_axis(csum, ids[:, None], axis=1)[:, 0] - 1
    counts = csum[-1]
    padded = ((counts + tile - 1) // tile) * tile
    ends = jnp.cumsum(padded)
    starts = ends - padded
    dest = starts[ids] + rank
    token = jnp.arange(2 * n, dtype=jnp.int32) // 2
    row_token = jnp.zeros((n_rows,), jnp.int32).at[dest].set(token)
    tile_start = jnp.arange(n_rows // tile, dtype=jnp.int32) * tile
    tile_expert = jnp.sum((tile_start[:, None] >= ends[None, :]).astype(jnp.int32), axis=1)
    e_c = jnp.minimum(tile_expert, N_EXPERTS - 1)
    tile_valid = jnp.where(tile_expert < N_EXPERTS,
                           jnp.clip(counts[e_c] - (tile_start - starts[e_c]), 0, tile), 0).astype(jnp.int32)
    last = jnp.max(jnp.where(counts > 0, jnp.arange(N_EXPERTS, dtype=jnp.int32), 0))
    tile_expert = jnp.where(tile_valid > 0, tile_expert, last).astype(jnp.int32)
    return tile_expert, tile_valid, row_token, dest


def _combine_kernel(dest_ref, x1_ref, route_ref, y_hbm, nw_ref, o_ref, ybuf, sem, *, tm):
    t = pl.program_id(0)
    nt = pl.num_programs(0)
    slot = t % 2

    def row_copy(tile, r, slot):
        src = dest_ref[tile * (2 * tm) + r]
        return pltpu.make_async_copy(y_hbm.at[src], ybuf.at[slot, :, r, :], sem.at[slot])

    def issue(tile, slot):
        def body(r, carry):
            row_copy(tile, r, slot).start()
            return carry
        lax.fori_loop(0, 2 * tm, body, 0, unroll=8)

    @pl.when(t == 0)
    def _():
        issue(0, 0)

    @pl.when(t + 1 < nt)
    def _():
        issue(t + 1, 1 - slot)

    def wait_body(r, carry):
        row_copy(t, r, slot).wait()
        return carry
    lax.fori_loop(0, 2 * tm, wait_body, 0, unroll=8)

    route = route_ref[...]
    y0 = _load_planes(ybuf.at[slot], slice(0, tm))
    y1 = _load_planes(ybuf.at[slot], slice(tm, 2 * tm))
    x = x1_ref[...] + (route[:, 2:3] * y0 + route[:, 3:4] * y1)
    ms = jnp.mean(x * x, axis=-1, keepdims=True)
    o_ref[...] = x * lax.rsqrt(ms + EPS) * nw_ref[...]


COMBINE_TILE = 128


def _combine(x1, route, y_sorted, dest_tiled, norm_w):
    n = x1.shape[0]
    tm = min(n, COMBINE_TILE)
    grid_spec = pltpu.PrefetchScalarGridSpec(
        num_scalar_prefetch=1,
        grid=(n // tm,),
        in_specs=[
            pl.BlockSpec((tm, D_MODEL), lambda t, d: (t, 0)),
            pl.BlockSpec((tm, LANES), lambda t, d: (t, 0)),
            pl.BlockSpec(memory_space=pl.ANY),
            pl.BlockSpec((1, D_MODEL), lambda t, d: (0, 0)),
        ],
        out_specs=pl.BlockSpec((tm, D_MODEL), lambda t, d: (t, 0)),
        scratch_shapes=[pltpu.VMEM((2, ROW_TILES, 2 * tm, LANES), F32), pltpu.SemaphoreType.DMA((2,))],
    )
    return pl.pallas_call(
        functools.partial(_combine_kernel, tm=tm),
        grid_spec=grid_spec,
        out_shape=jax.ShapeDtypeStruct((n, D_MODEL), F32),
        compiler_params=_params("arbitrary"),
        name="combine",
    )(dest_tiled, x1, route, y_sorted, norm_w)


def _ffn_and_final(x1, hn, route, wg, wu, wd, norm_final):
    n = x1.shape[0]
    tile = min(MOE_TILE, max(MOE_TILE_MIN, n // 4))
    tile_expert, tile_valid, row_token, dest = _moe_plan(route, tile)
    y_sorted = _moe(hn, tile_expert, tile_valid, row_token, wg, wu, wd)
    tm = min(n, COMBINE_TILE)
    dest_tiled = dest.reshape(n // tm, tm, 2).transpose(0, 2, 1).reshape(-1)
    return _combine(x1, route, y_sorted, dest_tiled, norm_final)


def _kmean_kernel(pt_ref, *refs, pages_per_step):
    page_refs = refs[:pages_per_step]
    o_ref = refs[pages_per_step]
    j = pl.program_id(1)
    ppb = MOBA_BLOCK // PAGE_SIZE
    for blk in range(pages_per_step // ppb):
        tot = jnp.sum(page_refs[blk * ppb][0], axis=0)
        for pg in range(1, ppb):
            tot = tot + jnp.sum(page_refs[blk * ppb + pg][0], axis=0)
        o_ref[0, j * (pages_per_step // ppb) + blk] = tot * (1.0 / MOBA_BLOCK)


def _kmean(cache_k, page_table):
    dec_batch, n_pages = page_table.shape
    ppb = MOBA_BLOCK // PAGE_SIZE
    n_blocks = n_pages // ppb
    pages_per_step = 16
    steps = n_pages // pages_per_step

    def page_spec(p):
        return pl.BlockSpec((1, PAGE_SIZE, HEADS, HEAD_W),
                            lambda s, j, pt, p=p: (pt[s * n_pages + j * pages_per_step + p], 0, 0, 0))

    grid_spec = pltpu.PrefetchScalarGridSpec(
        num_scalar_prefetch=1,
        grid=(dec_batch, steps),
        in_specs=[page_spec(p) for p in range(pages_per_step)],
        out_specs=pl.BlockSpec((1, n_blocks, HEADS, HEAD_W), lambda s, j, pt: (s, 0, 0, 0)),
    )
    return pl.pallas_call(
        functools.partial(_kmean_kernel, pages_per_step=pages_per_step),
        grid_spec=grid_spec,
        out_shape=jax.ShapeDtypeStruct((dec_batch, n_blocks, HEADS, HEAD_W), F32),
        compiler_params=_params("arbitrary", "arbitrary"),
        name="kmean",
    )(page_table.reshape(-1), *([cache_k] * pages_per_step))


def _sample_topk_kernel(q_ref, kmean_ref, idx_ref, q_scr, *, t_new, n_sel):
    q_scr[...] = jnp.zeros_like(q_scr)
    for h in range(HEADS):
        sl = slice(h * HEAD_W, (h + 1) * HEAD_W)
        q_scr[0:t_new, :] = q_ref[0, :, sl]
        gate = lax.dot_general(kmean_ref[0, :, h, :], q_scr[...], _NT, precision=HIGHEST,
                               preferred_element_type=F32)
        nb = gate.shape[0]
        blk = lax.broadcasted_iota(jnp.int32, gate.shape, 0)
        row = lax.broadcasted_iota(jnp.int32, (8, LANES), 0)
        tile = jnp.zeros((8, LANES), jnp.int32)
        for r in range(n_sel):
            mx = jnp.max(gate, axis=0, keepdims=True)
            first = jnp.min(jnp.where(gate == mx, blk, nb), axis=0, keepdims=True)
            tile = jnp.where(row == r, jnp.broadcast_to(first, (8, LANES)), tile)
            gate = jnp.where(blk == first, -jnp.inf, gate)
        idx_ref[0, h] = tile


def _sample_topk(main3, kmean, *, n_sel):
    dec_batch, t_new, _ = main3.shape
    n_blocks = kmean.shape[1]
    return pl.pallas_call(
        functools.partial(_sample_topk_kernel, t_new=t_new, n_sel=n_sel),
        grid=(dec_batch,),
        in_specs=[
            pl.BlockSpec((1, t_new, GROUP_W), lambda s: (s, 0, 4)),
            pl.BlockSpec((1, n_blocks, HEADS, HEAD_W), lambda s: (s, 0, 0, 0)),
        ],
        out_specs=pl.BlockSpec((1, HEADS, 8, LANES), lambda s: (s, 0, 0, 0)),
        out_shape=jax.ShapeDtypeStruct((dec_batch, HEADS, 8, LANES), jnp.int32),
        scratch_shapes=[pltpu.VMEM((LANES, HEAD_W), F32)],
        compiler_params=_params("arbitrary"),
        name="sample_topk",
    )(main3, kmean)


def _sample_attn_kernel(idx_ref, pt_ref, q_ref, kn_ref, vn_ref, ck_hbm, cv_hbm, o_ref,
                        kbuf, vbuf, q_scr, kn_scr, vn_scr, sem, *, t_new, n_sel, n_pages):
    s = pl.program_id(0)
    h = pl.program_id(1)
    step = s * HEADS + h
    n_steps = pl.num_programs(0) * HEADS
    slot = step % 2
    ppb = MOBA_BLOCK // PAGE_SIZE
    scale = HEAD_W ** -0.5

    def copies(step, slot):
        s_ = step // HEADS
        h_ = step % HEADS
        out = []
        for t in range(t_new):
            for r in range(n_sel):
                b = idx_ref[(step * t_new + t) * n_sel + r]
                for pg in range(ppb):
                    page = pt_ref[s_ * n_pages + b * ppb + pg]
                    rows = pl.ds((r * ppb + pg) * PAGE_SIZE, PAGE_SIZE)
                    out.append(pltpu.make_async_copy(ck_hbm.at[page, :, h_, :], kbuf.at[slot, t, rows, :],
                                                     sem.at[slot]))
                    out.append(pltpu.make_async_copy(cv_hbm.at[page, :, h_, :], vbuf.at[slot, t, rows, :],
                                                     sem.at[slot]))
        return out

    @pl.when(step == 0)
    def _():
        for c in copies(0, 0):
            c.start()

    @pl.when(step + 1 < n_steps)
    def _():
        for c in copies(step + 1, 1 - slot):
            c.start()

    q_scr[...] = jnp.zeros_like(q_scr)
    kn_scr[...] = jnp.zeros_like(kn_scr)
    vn_scr[...] = jnp.zeros_like(vn_scr)
    q_scr[0:t_new, :] = q_ref[0]
    kn_scr[0:t_new, :] = kn_ref[0]
    vn_scr[0:t_new, :] = vn_ref[0]
    mm = _mm3
    qb = q_scr[...]
    s_new = mm(qb, kn_scr[...], _NT) * scale
    trow = lax.broadcasted_iota(jnp.int32, s_new.shape, 0)
    jcol = lax.broadcasted_iota(jnp.int32, s_new.shape, 1)
    new_ok = (jcol <= trow) & (jcol < t_new)
    s_new = jnp.where(new_ok, s_new, NEG_BIG)
    m_new = jnp.max(s_new, axis=-1, keepdims=True)

    for c in copies(step, slot):
        c.wait()

    n_keys = n_sel * MOBA_BLOCK
    st = mm(qb, kbuf[slot].reshape(t_new * n_keys, HEAD_W), _NT) * scale
    srow = lax.broadcasted_iota(jnp.int32, st.shape, 0)
    scol = lax.broadcasted_iota(jnp.int32, st.shape, 1)
    own = (scol >= srow * n_keys) & (scol < (srow + 1) * n_keys)
    st = jnp.where(own, st, NEG_BIG)
    m = jnp.maximum(jnp.max(st, axis=-1, keepdims=True), m_new)
    p = jnp.exp(st - m)
    pn = jnp.where(new_ok, jnp.exp(s_new - m), 0.0)
    den = jnp.sum(p, axis=-1, keepdims=True) + jnp.sum(pn, axis=-1, keepdims=True)
    out = (mm(p, vbuf[slot].reshape(t_new * n_keys, HEAD_W)) + mm(pn, vn_scr[...])) / den
    o_ref[0] = out[0:t_new, :]


def _sample_attn(idx, page_table, main3, kn3, vn3, cache_k4, cache_v4, *, n_sel):
    dec_batch, t_new, _ = main3.shape
    n_pages = page_table.shape[1]
    q_col0 = 4 * HEADS
    grid_spec = pltpu.PrefetchScalarGridSpec(
        num_scalar_prefetch=2,
        grid=(dec_batch, HEADS),
        in_specs=[
            pl.BlockSpec((1, t_new, HEAD_W), lambda s, h, ix, pt: (s, 0, q_col0 + h)),
            pl.BlockSpec((1, t_new, HEAD_W), lambda s, h, ix, pt: (s, 0, h)),
            pl.BlockSpec((1, t_new, HEAD_W), lambda s, h, ix, pt: (s, 0, h)),
            pl.BlockSpec(memory_space=pl.ANY),
            pl.BlockSpec(memory_space=pl.ANY),
        ],
        out_specs=pl.BlockSpec((1, t_new, HEAD_W), lambda s, h, ix, pt: (s, 0, h)),
        scratch_shapes=[
            pltpu.VMEM((2, t_new, n_sel * MOBA_BLOCK, HEAD_W), F32),
            pltpu.VMEM((2, t_new, n_sel * MOBA_BLOCK, HEAD_W), F32),
            pltpu.VMEM((8, HEAD_W), F32),
            pltpu.VMEM((16, HEAD_W), F32),
            pltpu.VMEM((16, HEAD_W), F32),
            pltpu.SemaphoreType.DMA((2,)),
        ],
    )
    return pl.pallas_call(
        functools.partial(_sample_attn_kernel, t_new=t_new, n_sel=n_sel, n_pages=n_pages),
        grid_spec=grid_spec,
        out_shape=jax.ShapeDtypeStruct((dec_batch, t_new, GROUP_W), F32),
        compiler_params=_params("arbitrary", "arbitrary"),
        name="sample_attn",
    )(idx, page_table.reshape(-1), main3, kn3, vn3, cache_k4, cache_v4)


def _moba_sample(main, kn, vn, cache_k, cache_v, page_table):
    dec_batch, n_pages = page_table.shape
    t_new = main.shape[0] // dec_batch
    n_blocks = n_pages * PAGE_SIZE // MOBA_BLOCK
    n_sel = min(MOBA_TOPK, n_blocks)
    main3 = main.reshape(dec_batch, t_new, MAIN_W)
    kn3 = kn.reshape(dec_batch, t_new, GROUP_W)
    vn3 = vn.reshape(dec_batch, t_new, GROUP_W)
    kmean = _kmean(cache_k, page_table)
    idx = _sample_topk(main3, kmean, n_sel=n_sel)
    idx = idx[:, :, :n_sel, :t_new].transpose(0, 1, 3, 2).reshape(-1)
    o = _sample_attn(idx, page_table, main3, kn3, vn3, cache_k, cache_v, n_sel=n_sel)
    return o.reshape(dec_batch * t_new, GROUP_W)


def _layer(x, attend, hgrn_fn, weights):
    norm_mix, w_in, lbv, hgrn_norm, w_out, norm_ffn, w_router, wg, wu, wd, norm_final = weights
    main, k, v = _inproj(x, norm_mix, w_in)
    oh, state = hgrn_fn(main, lbv, hgrn_norm)
    oa = attend(main, k, v)
    x1, hn, route = _outproj(x, oh, oa, w_out, norm_ffn, w_router)
    y = _ffn_and_final(x1, hn, route, wg, wu, wd, norm_final)
    return y, k, v, state


def kernel(x_prompt, x_sample, cache_k, cache_v, state_hgrn, page_table, norm_mix, w_in, lb_logits, hgrn_norm,
           w_out, norm_ffn, w_group, w_expert, w_gate, w_up, w_down, norm_final):
    depth = w_in.shape[0]
    assert depth == 1, "the final norm is fused into the (single) layer"
    batch, seq, _ = x_prompt.shape
    dec_batch, t_new, _ = x_sample.shape
    lb_all = jnp.cumsum(jax.nn.softmax(lb_logits.astype(F32), axis=0), axis=0)
    layer = 0
    w_router = jnp.concatenate(
        [w_group[layer], w_expert[layer], jnp.zeros((D_MODEL, LANES - N_GROUPS - N_EXPERTS), F32)], axis=1)
    def weights(mix_dtype):
        return (
            norm_mix[layer][None, :], w_in[layer].astype(mix_dtype), lb_all[layer][None, :],
            hgrn_norm[layer][None, :], w_out[layer].astype(mix_dtype), norm_ffn[layer][None, :], w_router,
            w_gate[layer].reshape(N_EXPERTS, D_MODEL, EXPERT_FF).astype(BF16),
            w_up[layer].reshape(N_EXPERTS, D_MODEL, EXPERT_FF).astype(BF16),
            w_down[layer].reshape(N_EXPERTS, EXPERT_FF, D_MODEL).astype(BF16),
            norm_final[None, :],
        )

    hgrn_p = functools.partial(_hgrn, s0=None, batch=batch, chunk=HGRN_CHUNK, sub=HGRN_SUB, t_valid=HGRN_CHUNK,
                               precise=False)
    moba_p = functools.partial(_moba_prompt, batch=batch)
    yp, kp, vp, sp = _layer(x_prompt.reshape(batch * seq, D_MODEL), moba_p, hgrn_p, weights(BF16))

    pad = 8

    def hgrn_s(main, lbv, nw):
        m3 = main.reshape(dec_batch, t_new, MAIN_W)
        m3 = jnp.pad(m3, ((0, 0), (0, pad - t_new), (0, 0))).reshape(dec_batch * pad, MAIN_W)
        o, st = _hgrn(m3, lbv, nw, state_hgrn[layer], batch=dec_batch, chunk=pad, sub=pad, t_valid=t_new,
                      precise=True)
        return o.reshape(dec_batch, pad, GROUP_W)[:, :t_new].reshape(dec_batch * t_new, GROUP_W), st

    moba_s = functools.partial(_moba_sample, cache_k=cache_k.reshape(cache_k.shape[1:]),
                               cache_v=cache_v.reshape(cache_v.shape[1:]), page_table=page_table)
    ys, ks, vs, ss = _layer(x_sample.reshape(dec_batch * t_new, D_MODEL), moba_s, hgrn_s, weights(F32))

    return (yp.reshape(batch, seq, D_MODEL), ys.reshape(dec_batch, t_new, D_MODEL),
            kp.reshape(1, batch, seq, HEADS, HEAD_W), vp.reshape(1, batch, seq, HEADS, HEAD_W), sp[None],
            ks.reshape(1, dec_batch, t_new, HEADS, HEAD_W), vs.reshape(1, dec_batch, t_new, HEADS, HEAD_W), ss[None])
```
